```python
import math
import jax, jax.numpy as jnp
from jax import lax
import numpy as np

D_MODEL = 1024
BATCH = 4
SEQ = 8192
DEPTH = 1

D_MIX = D_MODEL
MLA_HEADS = 8
MLA_NOPE = 64
MLA_ROPE = 32
MLA_V = 64
MLA_WIDTH = MLA_HEADS * MLA_V
Q_LORA = 256
KV_LORA = 128
ROPE_THETA = 10000.0
Q_BLOCK = 128
CHUNK = 128
G_HEADS = 8
G_WIDTH = D_MIX - MLA_WIDTH
G_HEAD_DIM = G_WIDTH // G_HEADS
D_IN = Q_LORA + KV_LORA + MLA_ROPE + MLA_WIDTH + 3 * G_WIDTH
DN_ALPHA = (2.0 * DEPTH) ** 0.25
DN_BETA = (8.0 * DEPTH) ** -0.25
EPS = 1e-5

kernel_name = "hybrid_mla_gmlp_parallel_deepnorm"


def _rmsnorm(x, g):
    xf = x.astype(jnp.float32)
    y = xf * lax.rsqrt(jnp.mean(xf * xf, axis=-1, keepdims=True) + EPS)
    return (y * g.astype(jnp.float32)).astype(x.dtype)


def _layernorm(x, g, b):
    xf = x.astype(jnp.float32)
    mu = jnp.mean(xf, axis=-1, keepdims=True)
    var = jnp.mean(jnp.square(xf - mu), axis=-1, keepdims=True)
    y = (xf - mu) * lax.rsqrt(var + EPS)
    return (y * g.astype(jnp.float32) + b.astype(jnp.float32)).astype(x.dtype)


def _rope(t, positions):
    half = MLA_ROPE // 2
    inv_freq = 1.0 / (ROPE_THETA ** (jnp.arange(half, dtype=jnp.float32) / half))
    ang = positions.astype(jnp.float32)[..., None] * inv_freq
    cos = jnp.cos(ang)[:, :, None, :].astype(t.dtype)
    sin = jnp.sin(ang)[:, :, None, :].astype(t.dtype)
    t1, t2 = t[..., :half], t[..., half:]
    return jnp.concatenate([t1 * cos - t2 * sin, t1 * sin + t2 * cos], axis=-1)


def _causal_attention(q, k, v):
    b, s, h, dqk = q.shape
    dv = v.shape[-1]
    nb = s // Q_BLOCK
    scale = 1.0 / math.sqrt(dqk)
    qb = q.reshape(b, nb, Q_BLOCK, h, dqk).transpose(1, 0, 2, 3, 4)
    kpos = jnp.arange(s)

    def one_block(args):
        qi, i = args
        sc = jnp.einsum('bqhd,bkhd->bhqk', qi, k).astype(jnp.float32) * scale
        qpos = i * Q_BLOCK + jnp.arange(Q_BLOCK)
        mask = kpos[None, :] <= qpos[:, None]
        sc = jnp.where(mask[None, None], sc, -jnp.inf)
        p = jax.nn.softmax(sc, axis=-1).astype(v.dtype)
        return jnp.einsum('bhqk,bkhd->bqhd', p, v)

    o = lax.map(one_block, (qb, jnp.arange(nb)))
    return o.transpose(1, 0, 2, 3, 4).reshape(b, s, h, dv)


def setup_inputs(seed: int = 0) -> dict:
    key = jax.random.key(seed)
    ks = jax.random.split(key, 16)
    f32 = jnp.float32
    x = jax.random.normal(ks[0], (BATCH, SEQ, D_MODEL), f32)
    positions = jnp.broadcast_to(jnp.arange(SEQ, dtype=jnp.int32)[None, :], (BATCH, SEQ))
    w_in = jax.random.normal(ks[1], (D_MODEL, D_IN), f32) * D_MODEL ** -0.5
    q_norm_g = 1.0 + 0.02 * jax.random.normal(ks[2], (Q_LORA,), f32)
    w_uq = jax.random.normal(ks[3], (Q_LORA, MLA_HEADS * (MLA_NOPE + MLA_ROPE)), f32) * Q_LORA ** -0.5
    kv_norm_g = 1.0 + 0.02 * jax.random.normal(ks[4], (KV_LORA,), f32)
    w_ukv = jax.random.normal(ks[5], (KV_LORA, MLA_HEADS * (MLA_NOPE + MLA_V)), f32) * KV_LORA ** -0.5
    sgu_norm_g = 1.0 + 0.02 * jax.random.normal(ks[6], (G_WIDTH,), f32)
    sgu_norm_b = 0.02 * jax.random.normal(ks[7], (G_WIDTH,), f32)
    w_spatial = jax.random.normal(ks[8], (G_HEADS, CHUNK, CHUNK), f32) * CHUNK ** -0.5
    b_spatial = 1.0 + 0.02 * jax.random.normal(ks[9], (G_HEADS, CHUNK), f32)
    w_out = jax.random.normal(ks[10], (D_MIX, D_MODEL), f32) * (D_MIX ** -0.5) * DN_BETA
    ln_g = 1.0 + 0.02 * jax.random.normal(ks[11], (D_MODEL,), f32)
    ln_b = 0.02 * jax.random.normal(ks[12], (D_MODEL,), f32)
    return {"x": x, "positions": positions, "w_in": w_in, "q_norm_g": q_norm_g,
            "w_uq": w_uq, "kv_norm_g": kv_norm_g, "w_ukv": w_ukv,
            "sgu_norm_g": sgu_norm_g, "sgu_norm_b": sgu_norm_b,
            "w_spatial": w_spatial, "b_spatial": b_spatial, "w_out": w_out,
            "ln_g": ln_g, "ln_b": ln_b}


def _hybrid_mixer(h, positions, w_in, q_norm_g, w_uq, kv_norm_g, w_ukv,
                  sgu_norm_g, sgu_norm_b, w_spatial, b_spatial, w_out):
    b, s, _ = h.shape
    proj = jnp.einsum('bsd,de->bse', h, w_in)
    splits = np.cumsum([Q_LORA, KV_LORA, MLA_ROPE, MLA_WIDTH, G_WIDTH, G_WIDTH]).tolist()
    c_q, c_kv, k_rope, z_a, u, v = jnp.split(proj, splits, axis=-1)[:6]
    z_b = proj[..., splits[-1]:]

    q = jnp.einsum('bsr,re->bse', _rmsnorm(c_q, q_norm_g), w_uq)
    q = q.reshape(b, s, MLA_HEADS, MLA_NOPE + MLA_ROPE)
    q_nope, q_rope = q[..., :MLA_NOPE], _rope(q[..., MLA_NOPE:], positions)
    kv = jnp.einsum('bsr,re->bse', _rmsnorm(c_kv, kv_norm_g), w_ukv)
    kv = kv.reshape(b, s, MLA_HEADS, MLA_NOPE + MLA_V)
    k_nope, val = kv[..., :MLA_NOPE], kv[..., MLA_NOPE:]
    k_r = jnp.broadcast_to(_rope(k_rope[:, :, None, :], positions), (b, s, MLA_HEADS, MLA_ROPE))
    qf = jnp.concatenate([q_nope, q_rope], axis=-1)
    kf = jnp.concatenate([k_nope, k_r], axis=-1)
    attn = _causal_attention(qf, kf, val).reshape(b, s, MLA_WIDTH)
    out_a = attn * jax.nn.silu(z_a)

    u = jax.nn.gelu(u, approximate=False)
    v = _layernorm(jax.nn.gelu(v, approximate=False), sgu_norm_g, sgu_norm_b)
    nc = s // CHUNK
    vc = v.reshape(b, nc, CHUNK, G_HEADS, G_HEAD_DIM)
    causal = jnp.tril(jnp.ones((CHUNK, CHUNK), dtype=bool))
    w_s = jnp.where(causal[None], w_spatial, 0.0).astype(v.dtype)
    sv = jnp.einsum('hts,bcshd->bcthd', w_s, vc) + b_spatial.T[None, None, :, :, None]
    sgu = u * sv.reshape(b, s, G_WIDTH)
    out_b = sgu * jax.nn.silu(z_b)

    merged = jnp.concatenate([out_a, out_b], axis=-1)
    return jnp.einsum('bse,ed->bsd', merged, w_out)


def reference(x, positions, w_in, q_norm_g, w_uq, kv_norm_g, w_ukv,
              sgu_norm_g, sgu_norm_b, w_spatial, b_spatial, w_out, ln_g, ln_b):
    h = x
    for _ in range(DEPTH):
        y = _hybrid_mixer(h, positions, w_in, q_norm_g, w_uq, kv_norm_g, w_ukv,
                          sgu_norm_g, sgu_norm_b, w_spatial, b_spatial, w_out)
        h = _layernorm(DN_ALPHA * h + y, ln_g, ln_b)
    return h
```

```python
import functools
import math

import jax
import jax.numpy as jnp
import numpy as np
from jax import lax
from jax.experimental import pallas as pl
from jax.experimental.pallas import tpu as pltpu

D_MODEL = 1024
HEADS = 8
NOPE = 64
ROPE = 32
HALF = ROPE // 2
V_DIM = 64
A_WIDTH = HEADS * V_DIM
Q_LORA = 256
KV_LORA = 128
ROPE_THETA = 10000.0
CHUNK = 128
G_HEADS = 8
G_WIDTH = 512
G_HEAD_DIM = G_WIDTH // G_HEADS
DN_ALPHA = 2.0 ** 0.25
EPS = 1e-5
SM_SCALE = 1.0 / math.sqrt(NOPE + ROPE)

LANES = 128
HP = HEADS * LANES
ROPE_LO = NOPE

C_CQ = 0
C_CKV = C_CQ + Q_LORA
C_KR = C_CKV + KV_LORA
C_ZA = C_KR + LANES
C_U = C_ZA + A_WIDTH
C_V = C_U + G_WIDTH
C_ZB = C_V + G_WIDTH
C_END = C_ZB + G_WIDTH

TM_PROJ = 512
TM_OUT = 512
TQ = 512
TK = 512
HEADS_PER_STEP = 2
NEG_BIG = -1e30

VMEM_LIMIT = 48 * 1024 * 1024

_f32 = jnp.float32
_bf16 = jnp.bfloat16


def _dot(a, b):
    return jnp.dot(a, b, preferred_element_type=_f32)


def _gelu(x):
    return 0.5 * x * (1.0 + lax.erf(x * (1.0 / math.sqrt(2.0))))


def _silu(x):
    return x * (1.0 / (1.0 + jnp.exp(-x)))


def _proj_kernel(x_ref, pos_ref, aux_ref, win_ref, gq_ref, wuq_ref, gkv_ref, wuk_ref,
                 wuv_ref, sg_ref, sb_ref, wsp_ref, bsp_ref,
                 q_out, k_out, v_out, ga_out, ob_out):
    tm = x_ref.shape[1]
    x = x_ref[0].astype(_bf16)

    def proj(lo, hi):
        return _dot(x, win_ref[:, lo:hi])

    pos = pos_ref[0].astype(_f32)
    ang = pos * aux_ref[0:1, :]
    cos = jnp.cos(ang)
    sin = jnp.sin(ang)
    sin_up = sin * aux_ref[1:2, :]
    sin_dn = sin * aux_ref[2:3, :]

    def rope(t):
        return (t * cos + pltpu.roll(t, HALF, 1) * sin_up
                + pltpu.roll(t, LANES - HALF, 1) * sin_dn)

    def rms(c, g_ref):
        return c * lax.rsqrt(jnp.mean(c * c, axis=-1, keepdims=True) + EPS) * g_ref[...]

    cqn = rms(proj(C_CQ, C_CKV), gq_ref).astype(_bf16)
    q = _dot(cqn, wuq_ref[...]) * SM_SCALE
    ckn = rms(proj(C_CKV, C_KR), gkv_ref).astype(_bf16)
    kn = _dot(ckn, wuk_ref[...])
    vv = _dot(ckn, wuv_ref[...])
    kr = rope(proj(C_KR, C_ZA))
    ones_col = aux_ref[3:4, :]
    for h in range(HEADS):
        sl = slice(h * LANES, (h + 1) * LANES)
        q_out[0, h] = rope(q[:, sl]).astype(_bf16)
        k_out[0, h] = (kn[:, sl] + kr).astype(_bf16)
        v_out[0, h] = (vv[:, sl] + ones_col).astype(_bf16)

    ga_out[0] = _silu(proj(C_ZA, C_U)).astype(_bf16)

    u = _gelu(proj(C_U, C_V))
    v = _gelu(proj(C_V, C_ZB))
    mu = jnp.mean(v, axis=-1, keepdims=True)
    vc = v - mu
    var = jnp.mean(vc * vc, axis=-1, keepdims=True)
    vn = (vc * lax.rsqrt(var + EPS) * sg_ref[...] + sb_ref[...])
    gb = _silu(proj(C_ZB, C_END))

    row = lax.broadcasted_iota(jnp.int32, (CHUNK, 2 * CHUNK), 0)
    col = lax.broadcasted_iota(jnp.int32, (CHUNK, 2 * CHUNK), 1)
    tri = (col % CHUNK) <= row
    lane = lax.broadcasted_iota(jnp.int32, (CHUNK, LANES), 1)
    lo_half = lane < G_HEAD_DIM
    for pair in range(G_HEADS // 2):
        w_pair = jnp.where(tri, wsp_ref[pair], 0.0).astype(_bf16)
        ls = slice(pair * LANES, (pair + 1) * LANES)
        for c in range(tm // CHUNK):
            rs = slice(c * CHUNK, (c + 1) * CHUNK)
            vp = vn[rs, ls]
            stacked = jnp.concatenate(
                [jnp.where(lo_half, vp, 0.0), jnp.where(lo_half, 0.0, vp)], axis=0
            ).astype(_bf16)
            sv = _dot(w_pair, stacked) + bsp_ref[:, ls]
            ob_out[0, rs, ls] = (u[rs, ls] * sv * gb[rs, ls]).astype(_bf16)


def _attn_kernel(q_ref, k_ref, v_ref, g_ref, o_ref, m_sc, acc_sc):
    qi = pl.program_id(2)
    tq = q_ref.shape[2]
    outs = []
    for hh in range(HEADS_PER_STEP):
        q = q_ref[0, hh]
        m_sc[...] = jnp.full(m_sc.shape, NEG_BIG, _f32)
        acc_sc[...] = jnp.zeros(acc_sc.shape, _f32)

        def step(j, masked, hh=hh, q=q):
            ks = pl.multiple_of(j * TK, TK)
            k = k_ref[0, hh, pl.ds(ks, TK), :]
            v = v_ref[0, hh, pl.ds(ks, TK), :]
            s = lax.dot_general(q, k, (((1,), (1,)), ((), ())),
                                preferred_element_type=_f32)
            if masked:
                r = lax.broadcasted_iota(jnp.int32, s.shape, 0)
                c = lax.broadcasted_iota(jnp.int32, s.shape, 1)
                s = jnp.where(c <= r, s, NEG_BIG)
            m_old = m_sc[...]
            m_new = jnp.maximum(m_old, jnp.max(s, axis=-1, keepdims=True))
            alpha = jnp.exp(m_old - m_new)
            p = jnp.exp(s - m_new).astype(_bf16)
            acc_sc[...] = alpha * acc_sc[...] + _dot(p, v)
            m_sc[...] = m_new

        def body(j, carry):
            step(j, False)
            return carry

        lax.fori_loop(0, qi, body, 0)
        step(qi, True)
        acc = acc_sc[...]
        outs.append(acc * (1.0 / acc[:, V_DIM:V_DIM + 1]))
    lane = lax.broadcasted_iota(jnp.int32, (tq, LANES), 1)
    both = jnp.where(lane < V_DIM, outs[0], pltpu.roll(outs[1], V_DIM, 1))
    o_ref[0] = (both * g_ref[0].astype(_f32)).astype(_bf16)


def _out_kernel(x_ref, oa_ref, ob_ref, wo_ref, g_ref, b_ref, o_ref):
    y = _dot(oa_ref[...], wo_ref[:A_WIDTH, :]) + _dot(ob_ref[...], wo_ref[A_WIDTH:, :])
    z = DN_ALPHA * x_ref[...] + y
    mu = jnp.mean(z, axis=-1, keepdims=True)
    zc = z - mu
    var = jnp.mean(zc * zc, axis=-1, keepdims=True)
    o_ref[...] = zc * lax.rsqrt(var + EPS) * g_ref[...] + b_ref[...]


def _prep_weights(w_in, w_uq, w_ukv, w_spatial, b_spatial):
    splits = np.cumsum([Q_LORA, KV_LORA, ROPE, A_WIDTH, G_WIDTH, G_WIDTH]).tolist()
    w_cq, w_ckv, w_kr, w_za, w_u, w_v = jnp.split(w_in, splits, axis=1)[:6]
    w_zb = w_in[:, splits[-1]:]
    kr_pad = jnp.pad(w_kr, ((0, 0), (ROPE_LO, LANES - ROPE_LO - ROPE)))
    win_p = jnp.concatenate([w_cq, w_ckv, kr_pad, w_za, w_u, w_v, w_zb], axis=1).astype(_bf16)

    wq = w_uq.reshape(Q_LORA, HEADS, NOPE + ROPE)
    wq = jnp.pad(wq, ((0, 0), (0, 0), (0, LANES - NOPE - ROPE))).reshape(Q_LORA, HP)
    wkv = w_ukv.reshape(KV_LORA, HEADS, NOPE + V_DIM)
    wk = jnp.pad(wkv[:, :, :NOPE], ((0, 0), (0, 0), (0, LANES - NOPE))).reshape(KV_LORA, HP)
    wv = jnp.pad(wkv[:, :, NOPE:], ((0, 0), (0, 0), (0, LANES - V_DIM))).reshape(KV_LORA, HP)

    wsp = w_spatial.reshape(G_HEADS // 2, 2, CHUNK, CHUNK).transpose(0, 2, 1, 3)
    wsp = wsp.reshape(G_HEADS // 2, CHUNK, 2 * CHUNK)
    bsp = jnp.repeat(b_spatial.T, G_HEAD_DIM, axis=1)
    return win_p, wq.astype(_bf16), wk.astype(_bf16), wv.astype(_bf16), wsp, bsp


def _aux_table():
    half = HALF
    inv_freq = 1.0 / (ROPE_THETA ** (jnp.arange(half, dtype=_f32) / half))
    lane = np.arange(LANES)
    in_rope = (lane >= ROPE_LO) & (lane < ROPE_LO + ROPE)
    freq = jnp.where(in_rope, jnp.tile(inv_freq, LANES // half), 0.0)
    up = ((lane >= ROPE_LO + half) & (lane < ROPE_LO + ROPE)).astype(np.float32)
    dn = -((lane >= ROPE_LO) & (lane < ROPE_LO + half)).astype(np.float32)
    one = (lane == V_DIM).astype(np.float32)
    zero = np.zeros((4, LANES), np.float32)
    return jnp.concatenate([freq[None], jnp.asarray(np.stack([up, dn, one])),
                            jnp.asarray(zero)], axis=0)


def kernel(x, positions, w_in, q_norm_g, w_uq, kv_norm_g, w_ukv, sgu_norm_g, sgu_norm_b,
           w_spatial, b_spatial, w_out, ln_g, ln_b):
    b, s, d = x.shape
    assert d == D_MODEL and s % TQ == 0 and s % TM_PROJ == 0 and TQ == TK
    win_p, wq, wk, wv, wsp, bsp = _prep_weights(w_in, w_uq, w_ukv, w_spatial, b_spatial)
    aux = _aux_table()
    pos3 = positions.reshape(b, s, 1)

    const2 = lambda shape: pl.BlockSpec(shape, lambda bi, i: (0, 0))
    const3 = lambda shape: pl.BlockSpec(shape, lambda bi, i: (0, 0, 0))
    head_spec = pl.BlockSpec((1, HEADS, TM_PROJ, LANES), lambda bi, i: (bi, 0, i, 0))
    row_spec = lambda w: pl.BlockSpec((1, TM_PROJ, w), lambda bi, i: (bi, i, 0))
    hshape = jax.ShapeDtypeStruct((b, HEADS, s, LANES), _bf16)
    qp, kp, vp, ga, ob = pl.pallas_call(
        _proj_kernel,
        grid=(b, s // TM_PROJ),
        in_specs=[row_spec(D_MODEL), row_spec(1), const2((8, LANES)),
                  const2((D_MODEL, C_END)), const2((1, Q_LORA)), const2((Q_LORA, HP)),
                  const2((1, KV_LORA)), const2((KV_LORA, HP)), const2((KV_LORA, HP)),
                  const2((1, G_WIDTH)), const2((1, G_WIDTH)),
                  const3((G_HEADS // 2, CHUNK, 2 * CHUNK)), const2((CHUNK, G_WIDTH))],
        out_specs=[head_spec, head_spec, head_spec, row_spec(A_WIDTH), row_spec(G_WIDTH)],
        out_shape=[hshape, hshape, hshape,
                   jax.ShapeDtypeStruct((b, s, A_WIDTH), _bf16),
                   jax.ShapeDtypeStruct((b, s, G_WIDTH), _bf16)],
        compiler_params=pltpu.CompilerParams(
            dimension_semantics=("arbitrary", "arbitrary"), vmem_limit_bytes=VMEM_LIMIT),
        name="hyb_proj",
    )(x, pos3, aux, win_p, q_norm_g.reshape(1, -1), wq, kv_norm_g.reshape(1, -1), wk, wv,
      sgu_norm_g.reshape(1, -1), sgu_norm_b.reshape(1, -1), wsp, bsp)

    hps = HEADS_PER_STEP
    oa = pl.pallas_call(
        _attn_kernel,
        grid=(b, HEADS // hps, s // TQ),
        in_specs=[pl.BlockSpec((1, hps, TQ, LANES), lambda bi, hp, qi: (bi, hp, qi, 0)),
                  pl.BlockSpec((1, hps, s, LANES), lambda bi, hp, qi: (bi, hp, 0, 0)),
                  pl.BlockSpec((1, hps, s, LANES), lambda bi, hp, qi: (bi, hp, 0, 0)),
                  pl.BlockSpec((1, TQ, LANES), lambda bi, hp, qi: (bi, qi, hp))],
        out_specs=pl.BlockSpec((1, TQ, LANES), lambda bi, hp, qi: (bi, qi, hp)),
        out_shape=jax.ShapeDtypeStruct((b, s, A_WIDTH), _bf16),
        scratch_shapes=[pltpu.VMEM((TQ, 1), _f32), pltpu.VMEM((TQ, LANES), _f32)],
        compiler_params=pltpu.CompilerParams(
            dimension_semantics=("arbitrary", "arbitrary", "arbitrary"),
            vmem_limit_bytes=VMEM_LIMIT),
        name="hyb_attn",
    )(qp, kp, vp, ga)

    r = b * s
    rows = lambda w: pl.BlockSpec((TM_OUT, w), lambda i: (i, 0))
    fixed = lambda shape: pl.BlockSpec(shape, lambda i: (0, 0))
    out = pl.pallas_call(
        _out_kernel,
        grid=(r // TM_OUT,),
        in_specs=[rows(D_MODEL), rows(A_WIDTH), rows(G_WIDTH), fixed((D_MODEL, D_MODEL)),
                  fixed((1, D_MODEL)), fixed((1, D_MODEL))],
        out_specs=rows(D_MODEL),
        out_shape=jax.ShapeDtypeStruct((r, D_MODEL), x.dtype),
        compiler_params=pltpu.CompilerParams(
            dimension_semantics=("arbitrary",), vmem_limit_bytes=VMEM_LIMIT),
        name="hyb_out",
    )(x.reshape(r, d), oa.reshape(r, A_WIDTH), ob.reshape(r, G_WIDTH), w_out.astype(_bf16),
      ln_g.reshape(1, -1), ln_b.reshape(1, -1))
    return out.reshape(b, s, d)
```

```python
import functools
import math

import jax
import jax.numpy as jnp
import numpy as np
from jax import lax
from jax.experimental import pallas as pl
from jax.experimental.pallas import tpu as pltpu

D_MODEL = 1024
HEADS = 8
NOPE = 64
ROPE = 32
HALF = ROPE // 2
V_DIM = 64
A_WIDTH = HEADS * V_DIM
Q_LORA = 256
KV_LORA = 128
ROPE_THETA = 10000.0
CHUNK = 128
G_HEADS = 8
G_WIDTH = 512
G_HEAD_DIM = G_WIDTH // G_HEADS
DN_ALPHA = 2.0 ** 0.25
EPS = 1e-5
SM_SCALE = 1.0 / math.sqrt(NOPE + ROPE)
Q_SCALE = SM_SCALE * math.log2(math.e)

LANES = 128
HP = HEADS * LANES
ROPE_LO = NOPE

C_CQ = 0
C_CKV = C_CQ + Q_LORA
C_KR = C_CKV + KV_LORA
C_ZA = C_KR + LANES
C_U = C_ZA + A_WIDTH
C_V = C_U + G_WIDTH
C_ZB = C_V + G_WIDTH
C_END = C_ZB + G_WIDTH

TM_PROJ = 512
TM_OUT = 512
TQ = 512
TK = 512
HEADS_PER_STEP = 2
NEG_BIG = -1e30

VMEM_LIMIT = 48 * 1024 * 1024

_f32 = jnp.float32
_bf16 = jnp.bfloat16


def _dot(a, b):
    return jnp.dot(a, b, preferred_element_type=_f32)


def _gelu(x):
    return 0.5 * x * (1.0 + lax.erf(x * (1.0 / math.sqrt(2.0))))


def _silu(x):
    return x * (1.0 / (1.0 + jnp.exp(-x)))


def _proj_kernel(x_ref, pos_ref, aux_ref, win_ref, gq_ref, wuq_ref, gkv_ref, wuk_ref,
                 wuv_ref, sg_ref, sb_ref, wsp_ref, bsp_ref,
                 q_out, k_out, v_out, ga_out, ob_out):
    tm = x_ref.shape[1]
    x = x_ref[0].astype(_bf16)

    def proj(lo, hi):
        return _dot(x, win_ref[:, lo:hi])

    pos = pos_ref[0].astype(_f32)
    ang = pos * aux_ref[0:1, :]
    cos = jnp.cos(ang)
    sin = jnp.sin(ang)
    sin_up = sin * aux_ref[1:2, :]
    sin_dn = sin * aux_ref[2:3, :]

    def rope(t):
        return (t * cos + pltpu.roll(t, HALF, 1) * sin_up
                + pltpu.roll(t, LANES - HALF, 1) * sin_dn)

    def rms(c, g_ref):
        return c * lax.rsqrt(jnp.mean(c * c, axis=-1, keepdims=True) + EPS) * g_ref[...]

    cqn = rms(proj(C_CQ, C_CKV), gq_ref).astype(_bf16)
    q = _dot(cqn, wuq_ref[...]) * Q_SCALE
    ckn = rms(proj(C_CKV, C_KR), gkv_ref).astype(_bf16)
    kn = _dot(ckn, wuk_ref[...])
    vv = _dot(ckn, wuv_ref[...])
    kr = rope(proj(C_KR, C_ZA))
    ones_col = aux_ref[3:4, :]
    for h in range(HEADS):
        sl = slice(h * LANES, (h + 1) * LANES)
        q_out[0, h] = rope(q[:, sl]).astype(_bf16)
        k_out[0, h] = (kn[:, sl] + kr).astype(_bf16)
        v_out[0, h] = (vv[:, sl] + ones_col).astype(_bf16)

    ga_out[0] = _silu(proj(C_ZA, C_U)).astype(_bf16)

    u = _gelu(proj(C_U, C_V))
    v = _gelu(proj(C_V, C_ZB))
    mu = jnp.mean(v, axis=-1, keepdims=True)
    vc = v - mu
    var = jnp.mean(vc * vc, axis=-1, keepdims=True)
    vn = (vc * lax.rsqrt(var + EPS) * sg_ref[...] + sb_ref[...])
    gb = _silu(proj(C_ZB, C_END))

    row = lax.broadcasted_iota(jnp.int32, (CHUNK, 2 * CHUNK), 0)
    col = lax.broadcasted_iota(jnp.int32, (CHUNK, 2 * CHUNK), 1)
    tri = (col % CHUNK) <= row
    lane = lax.broadcasted_iota(jnp.int32, (CHUNK, LANES), 1)
    lo_half = lane < G_HEAD_DIM
    for pair in range(G_HEADS // 2):
        w_pair = jnp.where(tri, wsp_ref[pair], 0.0).astype(_bf16)
        ls = slice(pair * LANES, (pair + 1) * LANES)
        for c in range(tm // CHUNK):
            rs = slice(c * CHUNK, (c + 1) * CHUNK)
            vp = vn[rs, ls]
            stacked = jnp.concatenate(
                [jnp.where(lo_half, vp, 0.0), jnp.where(lo_half, 0.0, vp)], axis=0
            ).astype(_bf16)
            sv = _dot(w_pair, stacked) + bsp_ref[:, ls]
            ob_out[0, rs, ls] = (u[rs, ls] * sv * gb[rs, ls]).astype(_bf16)


def _attn_kernel(q_ref, k_ref, v_ref, g_ref, o_ref, s_sc, m_sc, acc_sc):
    qi = pl.program_id(2)
    tq = q_ref.shape[2]

    def scores(hh, j):
        ks = pl.multiple_of(j * TK, TK)
        k = k_ref[0, hh, pl.ds(ks, TK), :]
        s_sc[hh] = lax.dot_general(q_ref[0, hh], k, (((1,), (1,)), ((), ())),
                                   preferred_element_type=_f32)

    def softmax_pv(hh, j, masked):
        ks = pl.multiple_of(j * TK, TK)
        s = s_sc[hh]
        if masked:
            r = lax.broadcasted_iota(jnp.int32, s.shape, 0)
            c = lax.broadcasted_iota(jnp.int32, s.shape, 1)
            s = jnp.where(c <= r, s, NEG_BIG)
        m_prev = m_sc[hh]
        m_next = jnp.maximum(m_prev, jnp.max(s, axis=1, keepdims=True))
        p = jnp.exp2(s - jnp.tile(m_next, (1, TK // LANES))).astype(_bf16)
        alpha = jnp.exp2(m_prev - m_next)
        v = v_ref[0, hh, pl.ds(ks, TK), :]
        acc_sc[hh] = alpha * acc_sc[hh] + _dot(p, v)
        m_sc[hh] = m_next

    m_sc[...] = jnp.full(m_sc.shape, NEG_BIG, _f32)
    acc_sc[...] = jnp.zeros(acc_sc.shape, _f32)
    scores(0, 0)

    def body(j, carry):
        scores(1, j)
        softmax_pv(0, j, False)
        scores(0, j + 1)
        softmax_pv(1, j, False)
        return carry

    lax.fori_loop(0, qi, body, 0)
    scores(1, qi)
    softmax_pv(0, qi, True)
    softmax_pv(1, qi, True)

    outs = []
    for hh in range(HEADS_PER_STEP):
        acc = acc_sc[hh]
        outs.append(acc * (1.0 / acc[:, V_DIM:V_DIM + 1]))
    lane = lax.broadcasted_iota(jnp.int32, (tq, LANES), 1)
    both = jnp.where(lane < V_DIM, outs[0], pltpu.roll(outs[1], V_DIM, 1))
    o_ref[0] = (both * g_ref[0].astype(_f32)).astype(_bf16)


def _out_kernel(x_ref, oa_ref, ob_ref, wo_ref, g_ref, b_ref, o_ref):
    y = _dot(oa_ref[...], wo_ref[:A_WIDTH, :]) + _dot(ob_ref[...], wo_ref[A_WIDTH:, :])
    z = DN_ALPHA * x_ref[...] + y
    mu = jnp.mean(z, axis=-1, keepdims=True)
    zc = z - mu
    var = jnp.mean(zc * zc, axis=-1, keepdims=True)
    o_ref[...] = zc * lax.rsqrt(var + EPS) * g_ref[...] + b_ref[...]


def _prep_weights(w_in, w_uq, w_ukv, w_spatial, b_spatial):
    splits = np.cumsum([Q_LORA, KV_LORA, ROPE, A_WIDTH, G_WIDTH, G_WIDTH]).tolist()
    w_cq, w_ckv, w_kr, w_za, w_u, w_v = jnp.split(w_in, splits, axis=1)[:6]
    w_zb = w_in[:, splits[-1]:]
    kr_pad = jnp.pad(w_kr, ((0, 0), (ROPE_LO, LANES - ROPE_LO - ROPE)))
    win_p = jnp.concatenate([w_cq, w_ckv, kr_pad, w_za, w_u, w_v, w_zb], axis=1).astype(_bf16)

    wq = w_uq.reshape(Q_LORA, HEADS, NOPE + ROPE)
    wq = jnp.pad(wq, ((0, 0), (0, 0), (0, LANES - NOPE - ROPE))).reshape(Q_LORA, HP)
    wkv = w_ukv.reshape(KV_LORA, HEADS, NOPE + V_DIM)
    wk = jnp.pad(wkv[:, :, :NOPE], ((0, 0), (0, 0), (0, LANES - NOPE))).reshape(KV_LORA, HP)
    wv = jnp.pad(wkv[:, :, NOPE:], ((0, 0), (0, 0), (0, LANES - V_DIM))).reshape(KV_LORA, HP)

    wsp = w_spatial.reshape(G_HEADS // 2, 2, CHUNK, CHUNK).transpose(0, 2, 1, 3)
    wsp = wsp.reshape(G_HEADS // 2, CHUNK, 2 * CHUNK)
    bsp = jnp.repeat(b_spatial.T, G_HEAD_DIM, axis=1)
    return win_p, wq.astype(_bf16), wk.astype(_bf16), wv.astype(_bf16), wsp, bsp


def _aux_table():
    half = HALF
    inv_freq = 1.0 / (ROPE_THETA ** (jnp.arange(half, dtype=_f32) / half))
    lane = np.arange(LANES)
    in_rope = (lane >= ROPE_LO) & (lane < ROPE_LO + ROPE)
    freq = jnp.where(in_rope, jnp.tile(inv_freq, LANES // half), 0.0)
    up = ((lane >= ROPE_LO + half) & (lane < ROPE_LO + ROPE)).astype(np.float32)
    dn = -((lane >= ROPE_LO) & (lane < ROPE_LO + half)).astype(np.float32)
    one = (lane == V_DIM).astype(np.float32)
    zero = np.zeros((4, LANES), np.float32)
    return jnp.concatenate([freq[None], jnp.asarray(np.stack([up, dn, one])),
                            jnp.asarray(zero)], axis=0)


def kernel(x, positions, w_in, q_norm_g, w_uq, kv_norm_g, w_ukv, sgu_norm_g, sgu_norm_b,
           w_spatial, b_spatial, w_out, ln_g, ln_b):
    b, s, d = x.shape
    assert d == D_MODEL and s % TQ == 0 and s % TM_PROJ == 0 and TQ == TK and HEADS_PER_STEP == 2
    win_p, wq, wk, wv, wsp, bsp = _prep_weights(w_in, w_uq, w_ukv, w_spatial, b_spatial)
    aux = _aux_table()
    pos3 = positions.reshape(b, s, 1)

    const2 = lambda shape: pl.BlockSpec(shape, lambda bi, i: (0, 0))
    const3 = lambda shape: pl.BlockSpec(shape, lambda bi, i: (0, 0, 0))
    head_spec = pl.BlockSpec((1, HEADS, TM_PROJ, LANES), lambda bi, i: (bi, 0, i, 0))
    row_spec = lambda w: pl.BlockSpec((1, TM_PROJ, w), lambda bi, i: (bi, i, 0))
    hshape = jax.ShapeDtypeStruct((b, HEADS, s, LANES), _bf16)
    qp, kp, vp, ga, ob = pl.pallas_call(
        _proj_kernel,
        grid=(b, s // TM_PROJ),
        in_specs=[row_spec(D_MODEL), row_spec(1), const2((8, LANES)),
                  const2((D_MODEL, C_END)), const2((1, Q_LORA)), const2((Q_LORA, HP)),
                  const2((1, KV_LORA)), const2((KV_LORA, HP)), const2((KV_LORA, HP)),
                  const2((1, G_WIDTH)), const2((1, G_WIDTH)),
                  const3((G_HEADS // 2, CHUNK, 2 * CHUNK)), const2((CHUNK, G_WIDTH))],
        out_specs=[head_spec, head_spec, head_spec, row_spec(A_WIDTH), row_spec(G_WIDTH)],
        out_shape=[hshape, hshape, hshape,
                   jax.ShapeDtypeStruct((b, s, A_WIDTH), _bf16),
                   jax.ShapeDtypeStruct((b, s, G_WIDTH), _bf16)],
        compiler_params=pltpu.CompilerParams(
            dimension_semantics=("arbitrary", "arbitrary"), vmem_limit_bytes=VMEM_LIMIT),
        name="hyb_proj",
    )(x, pos3, aux, win_p, q_norm_g.reshape(1, -1), wq, kv_norm_g.reshape(1, -1), wk, wv,
      sgu_norm_g.reshape(1, -1), sgu_norm_b.reshape(1, -1), wsp, bsp)

    hps = HEADS_PER_STEP
    oa = pl.pallas_call(
        _attn_kernel,
        grid=(b, HEADS // hps, s // TQ),
        in_specs=[pl.BlockSpec((1, hps, TQ, LANES), lambda bi, hp, qi: (bi, hp, qi, 0)),
                  pl.BlockSpec((1, hps, s, LANES), lambda bi, hp, qi: (bi, hp, 0, 0)),
                  pl.BlockSpec((1, hps, s, LANES), lambda bi, hp, qi: (bi, hp, 0, 0)),
                  pl.BlockSpec((1, TQ, LANES), lambda bi, hp, qi: (bi, qi, hp))],
        out_specs=pl.BlockSpec((1, TQ, LANES), lambda bi, hp, qi: (bi, qi, hp)),
        out_shape=jax.ShapeDtypeStruct((b, s, A_WIDTH), _bf16),
        scratch_shapes=[pltpu.VMEM((hps, TQ, TK), _f32), pltpu.VMEM((hps, TQ, LANES), _f32),
                        pltpu.VMEM((hps, TQ, LANES), _f32)],
        compiler_params=pltpu.CompilerParams(
            dimension_semantics=("arbitrary", "arbitrary", "arbitrary"),
            vmem_limit_bytes=VMEM_LIMIT),
        name="hyb_attn",
    )(qp, kp, vp, ga)

    r = b * s
    rows = lambda w: pl.BlockSpec((TM_OUT, w), lambda i: (i, 0))
    fixed = lambda shape: pl.BlockSpec(shape, lambda i: (0, 0))
    out = pl.pallas_call(
        _out_kernel,
        grid=(r // TM_OUT,),
        in_specs=[rows(D_MODEL), rows(A_WIDTH), rows(G_WIDTH), fixed((D_MODEL, D_MODEL)),
                  fixed((1, D_MODEL)), fixed((1, D_MODEL))],
        out_specs=rows(D_MODEL),
        out_shape=jax.ShapeDtypeStruct((r, D_MODEL), x.dtype),
        compiler_params=pltpu.CompilerParams(
            dimension_semantics=("arbitrary",), vmem_limit_bytes=VMEM_LIMIT),
        name="hyb_out",
    )(x.reshape(r, d), oa.reshape(r, A_WIDTH), ob.reshape(r, G_WIDTH), w_out.astype(_bf16),
      ln_g.reshape(1, -1), ln_b.reshape(1, -1))
    return out.reshape(b, s, d)
```

```python
import functools
import math

import jax
import jax.numpy as jnp
import numpy as np
from jax import lax
from jax.experimental import pallas as pl
from jax.experimental.pallas import tpu as pltpu

D_MODEL = 1024
HEADS = 8
NOPE = 64
ROPE = 32
HALF = ROPE // 2
POS_PER_ROW = 8
V_DIM = 64
A_WIDTH = HEADS * V_DIM
Q_LORA = 256
KV_LORA = 128
ROPE_THETA = 10000.0
CHUNK = 128
G_HEADS = 8
G_WIDTH = 512
G_HEAD_DIM = G_WIDTH // G_HEADS
DN_ALPHA = 2.0 ** 0.25
EPS = 1e-5
SM_SCALE = 1.0 / math.sqrt(NOPE + ROPE)
Q_SCALE = SM_SCALE * math.log2(math.e)

LANES = 128
HP = HEADS * LANES
ROPE_LO = NOPE

C_CQ = 0
C_CKV = C_CQ + Q_LORA
C_KR = C_CKV + KV_LORA
C_ZA = C_KR + LANES
C_U = C_ZA + A_WIDTH
C_V = C_U + G_WIDTH
C_ZB = C_V + G_WIDTH
C_END = C_ZB + G_WIDTH

TM_PROJ = 512
TM_OUT = 512
TQ = 1024
TK = 512
HEADS_PER_STEP = 2
NEG_BIG = -1e30

VMEM_LIMIT = 48 * 1024 * 1024

_f32 = jnp.float32
_bf16 = jnp.bfloat16


def _dot(a, b):
    return jnp.dot(a, b, preferred_element_type=_f32)


def _gelu(x):
    return 0.5 * x * (1.0 + lax.erf(x * (1.0 / math.sqrt(2.0))))


def _silu(x):
    return x * (1.0 / (1.0 + jnp.exp(-x)))


def _proj_kernel(x_ref, pos_ref, aux_ref, win_ref, gq_ref, wuq_ref, gkv_ref, wuk_ref,
                 wuv_ref, sg_ref, sb_ref, wsp_ref, bsp_ref,
                 q_out, k_out, v_out, ga_out, ob_out, cos_sc, sup_sc, sdn_sc):
    tm = x_ref.shape[1]
    x = x_ref[0].astype(_bf16)

    def proj(lo, hi):
        return _dot(x, win_ref[:, lo:hi])

    ang = pos_ref[0].astype(_f32) * aux_ref[0:1, :]
    cos8 = jnp.cos(ang)
    sin8 = jnp.sin(ang)
    lane = lax.broadcasted_iota(jnp.int32, cos8.shape, 1)
    in_lo = (lane >= ROPE_LO) & (lane < ROPE_LO + HALF)
    in_hi = (lane >= ROPE_LO + HALF) & (lane < ROPE_LO + ROPE)

    def place(t8, lane0, i):
        shift = (lane0 - HALF * i) % LANES
        return t8 if shift == 0 else pltpu.roll(t8, shift, 1)

    for i in range(POS_PER_ROW):
        rows = pl.ds(i, tm // POS_PER_ROW, stride=POS_PER_ROW)
        cos_sc[rows, :] = jnp.where(in_lo, place(cos8, ROPE_LO, i),
                                    jnp.where(in_hi, place(cos8, ROPE_LO + HALF, i), 1.0))
        sup_sc[rows, :] = jnp.where(in_hi, place(sin8, ROPE_LO + HALF, i), 0.0)
        sdn_sc[rows, :] = jnp.where(in_lo, -place(sin8, ROPE_LO, i), 0.0)

    def rope(t):
        return (t * cos_sc[...] + pltpu.roll(t, HALF, 1) * sup_sc[...]
                + pltpu.roll(t, LANES - HALF, 1) * sdn_sc[...])

    def rms(c, g_ref):
        return c * lax.rsqrt(jnp.mean(c * c, axis=-1, keepdims=True) + EPS) * g_ref[...]

    lat = proj(C_CQ, C_ZA)
    cqn = rms(lat[:, C_CQ:C_CKV], gq_ref).astype(_bf16)
    q = _dot(cqn, wuq_ref[...]) * Q_SCALE
    ckn = rms(lat[:, C_CKV:C_KR], gkv_ref).astype(_bf16)
    kn = _dot(ckn, wuk_ref[...])
    vv = _dot(ckn, wuv_ref[...])
    kr = rope(lat[:, C_KR:C_ZA])
    ones_col = aux_ref[1:2, :]
    for h in range(HEADS):
        sl = slice(h * LANES, (h + 1) * LANES)
        q_out[0, h] = rope(q[:, sl]).astype(_bf16)
        k_out[0, h] = (kn[:, sl] + kr).astype(_bf16)
        v_out[0, h] = (vv[:, sl] + ones_col).astype(_bf16)

    ga_out[0] = _silu(proj(C_ZA, C_U)).astype(_bf16)

    u = _gelu(proj(C_U, C_V))
    v = _gelu(proj(C_V, C_ZB))
    mu = jnp.mean(v, axis=-1, keepdims=True)
    vc = v - mu
    var = jnp.mean(vc * vc, axis=-1, keepdims=True)
    vn = (vc * lax.rsqrt(var + EPS) * sg_ref[...] + sb_ref[...])
    gb = _silu(proj(C_ZB, C_END))

    row = lax.broadcasted_iota(jnp.int32, (CHUNK, 2 * CHUNK), 0)
    col = lax.broadcasted_iota(jnp.int32, (CHUNK, 2 * CHUNK), 1)
    tri = (col % CHUNK) <= row
    lane = lax.broadcasted_iota(jnp.int32, (CHUNK, LANES), 1)
    lo_half = lane < G_HEAD_DIM
    for pair in range(G_HEADS // 2):
        w_pair = jnp.where(tri, wsp_ref[pair], 0.0).astype(_bf16)
        ls = slice(pair * LANES, (pair + 1) * LANES)
        for c in range(tm // CHUNK):
            rs = slice(c * CHUNK, (c + 1) * CHUNK)
            vp = vn[rs, ls]
            stacked = jnp.concatenate(
                [jnp.where(lo_half, vp, 0.0), jnp.where(lo_half, 0.0, vp)], axis=0
            ).astype(_bf16)
            sv = _dot(w_pair, stacked) + bsp_ref[:, ls]
            ob_out[0, rs, ls] = (u[rs, ls] * sv * gb[rs, ls]).astype(_bf16)


def _attn_kernel(q_ref, k_ref, v_ref, g_ref, o_ref, s_sc, m_sc, acc_sc):
    qi = pl.program_id(2)
    tq = q_ref.shape[2]
    kv_per_q = tq // TK
    full = pl.ds(0, tq)
    low = pl.ds(TK, tq - TK)

    def scores(hh, j, rows=full):
        ks = pl.multiple_of(j * TK, TK)
        k = k_ref[0, hh, pl.ds(ks, TK), :]
        s_sc[hh, rows, :] = lax.dot_general(q_ref[0, hh, rows, :], k, (((1,), (1,)), ((), ())),
                                            preferred_element_type=_f32)

    def softmax_pv(hh, j, rows=full, diag_row0=None):
        ks = pl.multiple_of(j * TK, TK)
        s = s_sc[hh, rows, :]
        if diag_row0 is not None:
            r = lax.broadcasted_iota(jnp.int32, s.shape, 0)
            c = lax.broadcasted_iota(jnp.int32, s.shape, 1)
            s = jnp.where(c <= r - diag_row0, s, NEG_BIG)
        m_prev = m_sc[hh, rows, :]
        m_next = jnp.maximum(m_prev, jnp.max(s, axis=1, keepdims=True))
        p = jnp.exp2(s - jnp.tile(m_next, (1, TK // LANES))).astype(_bf16)
        alpha = jnp.exp2(m_prev - m_next)
        v = v_ref[0, hh, pl.ds(ks, TK), :]
        acc_sc[hh, rows, :] = alpha * acc_sc[hh, rows, :] + _dot(p, v)
        m_sc[hh, rows, :] = m_next

    m_sc[...] = jnp.full(m_sc.shape, NEG_BIG, _f32)
    acc_sc[...] = jnp.zeros(acc_sc.shape, _f32)
    scores(0, 0)

    def body(j, carry):
        scores(1, j)
        softmax_pv(0, j)
        scores(0, j + 1)
        softmax_pv(1, j)
        return carry

    j0 = kv_per_q * qi
    lax.fori_loop(0, j0, body, 0)
    scores(1, j0)
    softmax_pv(0, j0, diag_row0=0)
    scores(0, j0 + 1, low)
    softmax_pv(1, j0, diag_row0=0)
    scores(1, j0 + 1, low)
    softmax_pv(0, j0 + 1, low, diag_row0=0)
    softmax_pv(1, j0 + 1, low, diag_row0=0)

    outs = []
    for hh in range(HEADS_PER_STEP):
        acc = acc_sc[hh]
        outs.append(acc * (1.0 / acc[:, V_DIM:V_DIM + 1]))
    lane = lax.broadcasted_iota(jnp.int32, (tq, LANES), 1)
    both = jnp.where(lane < V_DIM, outs[0], pltpu.roll(outs[1], V_DIM, 1))
    o_ref[0] = (both * g_ref[0].astype(_f32)).astype(_bf16)


def _out_kernel(x_ref, oa_ref, ob_ref, wo_ref, g_ref, b_ref, o_ref):
    y = _dot(oa_ref[...], wo_ref[:A_WIDTH, :]) + _dot(ob_ref[...], wo_ref[A_WIDTH:, :])
    z = DN_ALPHA * x_ref[...] + y
    mu = jnp.mean(z, axis=-1, keepdims=True)
    zc = z - mu
    var = jnp.mean(zc * zc, axis=-1, keepdims=True)
    o_ref[...] = zc * lax.rsqrt(var + EPS) * g_ref[...] + b_ref[...]


def _prep_weights(w_in, w_uq, w_ukv, w_spatial, b_spatial):
    splits = np.cumsum([Q_LORA, KV_LORA, ROPE, A_WIDTH, G_WIDTH, G_WIDTH]).tolist()
    w_cq, w_ckv, w_kr, w_za, w_u, w_v = jnp.split(w_in, splits, axis=1)[:6]
    w_zb = w_in[:, splits[-1]:]
    kr_pad = jnp.pad(w_kr, ((0, 0), (ROPE_LO, LANES - ROPE_LO - ROPE)))
    win_p = jnp.concatenate([w_cq, w_ckv, kr_pad, w_za, w_u, w_v, w_zb], axis=1).astype(_bf16)

    wq = w_uq.reshape(Q_LORA, HEADS, NOPE + ROPE)
    wq = jnp.pad(wq, ((0, 0), (0, 0), (0, LANES - NOPE - ROPE))).reshape(Q_LORA, HP)
    wkv = w_ukv.reshape(KV_LORA, HEADS, NOPE + V_DIM)
    wk = jnp.pad(wkv[:, :, :NOPE], ((0, 0), (0, 0), (0, LANES - NOPE))).reshape(KV_LORA, HP)
    wv = jnp.pad(wkv[:, :, NOPE:], ((0, 0), (0, 0), (0, LANES - V_DIM))).reshape(KV_LORA, HP)

    wsp = w_spatial.reshape(G_HEADS // 2, 2, CHUNK, CHUNK).transpose(0, 2, 1, 3)
    wsp = wsp.reshape(G_HEADS // 2, CHUNK, 2 * CHUNK)
    bsp = jnp.repeat(b_spatial.T, G_HEAD_DIM, axis=1)
    return win_p, wq.astype(_bf16), wk.astype(_bf16), wv.astype(_bf16), wsp, bsp


def _aux_table():
    inv_freq = 1.0 / (ROPE_THETA ** (jnp.arange(HALF, dtype=_f32) / HALF))
    freq = jnp.tile(inv_freq, POS_PER_ROW)
    one = jnp.asarray((np.arange(LANES) == V_DIM).astype(np.float32))
    return jnp.concatenate([freq[None], one[None], jnp.zeros((6, LANES), _f32)], axis=0)


def _pack_positions(positions):
    b, s = positions.shape
    return jnp.repeat(positions.reshape(b, s // POS_PER_ROW, POS_PER_ROW), HALF, axis=-1)


def kernel(x, positions, w_in, q_norm_g, w_uq, kv_norm_g, w_ukv, sgu_norm_g, sgu_norm_b,
           w_spatial, b_spatial, w_out, ln_g, ln_b):
    b, s, d = x.shape
    assert d == D_MODEL and s % TQ == 0 and s % TM_PROJ == 0 and TQ == 2 * TK and HEADS_PER_STEP == 2
    win_p, wq, wk, wv, wsp, bsp = _prep_weights(w_in, w_uq, w_ukv, w_spatial, b_spatial)
    aux = _aux_table()
    pos8 = _pack_positions(positions)

    const2 = lambda shape: pl.BlockSpec(shape, lambda bi, i: (0, 0))
    const3 = lambda shape: pl.BlockSpec(shape, lambda bi, i: (0, 0, 0))
    head_spec = pl.BlockSpec((1, HEADS, TM_PROJ, LANES), lambda bi, i: (bi, 0, i, 0))
    row_spec = lambda w: pl.BlockSpec((1, TM_PROJ, w), lambda bi, i: (bi, i, 0))
    hshape = jax.ShapeDtypeStruct((b, HEADS, s, LANES), _bf16)
    qp, kp, vp, ga, ob = pl.pallas_call(
        _proj_kernel,
        grid=(b, s // TM_PROJ),
        in_specs=[row_spec(D_MODEL),
                  pl.BlockSpec((1, TM_PROJ // POS_PER_ROW, LANES), lambda bi, i: (bi, i, 0)),
                  const2((8, LANES)),
                  const2((D_MODEL, C_END)), const2((1, Q_LORA)), const2((Q_LORA, HP)),
                  const2((1, KV_LORA)), const2((KV_LORA, HP)), const2((KV_LORA, HP)),
                  const2((1, G_WIDTH)), const2((1, G_WIDTH)),
                  const3((G_HEADS // 2, CHUNK, 2 * CHUNK)), const2((CHUNK, G_WIDTH))],
        out_specs=[head_spec, head_spec, head_spec, row_spec(A_WIDTH), row_spec(G_WIDTH)],
        out_shape=[hshape, hshape, hshape,
                   jax.ShapeDtypeStruct((b, s, A_WIDTH), _bf16),
                   jax.ShapeDtypeStruct((b, s, G_WIDTH), _bf16)],
        scratch_shapes=[pltpu.VMEM((TM_PROJ, LANES), _f32)] * 3,
        compiler_params=pltpu.CompilerParams(
            dimension_semantics=("arbitrary", "arbitrary"), vmem_limit_bytes=VMEM_LIMIT),
        name="hyb_proj",
    )(x, pos8, aux, win_p, q_norm_g.reshape(1, -1), wq, kv_norm_g.reshape(1, -1), wk, wv,
      sgu_norm_g.reshape(1, -1), sgu_norm_b.reshape(1, -1), wsp, bsp)

    hps = HEADS_PER_STEP
    oa = pl.pallas_call(
        _attn_kernel,
        grid=(b, HEADS // hps, s // TQ),
        in_specs=[pl.BlockSpec((1, hps, TQ, LANES), lambda bi, hp, qi: (bi, hp, qi, 0)),
                  pl.BlockSpec((1, hps, s, LANES), lambda bi, hp, qi: (bi, hp, 0, 0)),
                  pl.BlockSpec((1, hps, s, LANES), lambda bi, hp, qi: (bi, hp, 0, 0)),
                  pl.BlockSpec((1, TQ, LANES), lambda bi, hp, qi: (bi, qi, hp))],
        out_specs=pl.BlockSpec((1, TQ, LANES), lambda bi, hp, qi: (bi, qi, hp)),
        out_shape=jax.ShapeDtypeStruct((b, s, A_WIDTH), _bf16),
        scratch_shapes=[pltpu.VMEM((hps, TQ, TK), _f32), pltpu.VMEM((hps, TQ, LANES), _f32),
                        pltpu.VMEM((hps, TQ, LANES), _f32)],
        compiler_params=pltpu.CompilerParams(
            dimension_semantics=("arbitrary", "arbitrary", "arbitrary"),
            vmem_limit_bytes=VMEM_LIMIT),
        name="hyb_attn",
    )(qp, kp, vp, ga)

    r = b * s
    rows = lambda w: pl.BlockSpec((TM_OUT, w), lambda i: (i, 0))
    fixed = lambda shape: pl.BlockSpec(shape, lambda i: (0, 0))
    out = pl.pallas_call(
        _out_kernel,
        grid=(r // TM_OUT,),
        in_specs=[rows(D_MODEL), rows(A_WIDTH), rows(G_WIDTH), fixed((D_MODEL, D_MODEL)),
                  fixed((1, D_MODEL)), fixed((1, D_MODEL))],
        out_specs=rows(D_MODEL),
        out_shape=jax.ShapeDtypeStruct((r, D_MODEL), x.dtype),
        compiler_params=pltpu.CompilerParams(
            dimension_semantics=("arbitrary",), vmem_limit_bytes=VMEM_LIMIT),
        name="hyb_out",
    )(x.reshape(r, d), oa.reshape(r, A_WIDTH), ob.reshape(r, G_WIDTH), w_out.astype(_bf16),
      ln_g.reshape(1, -1), ln_b.reshape(1, -1))
    return out.reshape(b, s, d)
```

```python
import functools
import math

import jax
import jax.numpy as jnp
import numpy as np
from jax import lax
from jax.experimental import pallas as pl
from jax.experimental.pallas import tpu as pltpu

D_MODEL = 1024
HEADS = 8
NOPE = 64
ROPE = 32
HALF = ROPE // 2
POS_PER_ROW = 8
V_DIM = 64
A_WIDTH = HEADS * V_DIM
Q_LORA = 256
KV_LORA = 128
ROPE_THETA = 10000.0
CHUNK = 128
G_HEADS = 8
G_WIDTH = 512
G_HEAD_DIM = G_WIDTH // G_HEADS
DN_ALPHA = 2.0 ** 0.25
EPS = 1e-5
SM_SCALE = 1.0 / math.sqrt(NOPE + ROPE)
Q_SCALE = SM_SCALE * math.log2(math.e)

LANES = 128
HP = HEADS * LANES
ROPE_LO = NOPE

C_CQ = 0
C_CKV = C_CQ + Q_LORA
C_KR = C_CKV + KV_LORA
C_ZA = C_KR + LANES
C_U = C_ZA + A_WIDTH
C_V = C_U + G_WIDTH
C_ZB = C_V + G_WIDTH
C_END = C_ZB + G_WIDTH

TM_PROJ = 512
TM_OUT = 1024
TQ = 1024
TK = 512
HEADS_PER_STEP = 2
NEG_BIG = -1e30

VMEM_LIMIT = 48 * 1024 * 1024
VMEM_LIMIT_ATTN = 52 * 1024 * 1024

_f32 = jnp.float32
_bf16 = jnp.bfloat16


def _dot(a, b):
    return jnp.dot(a, b, preferred_element_type=_f32)


def _gelu(x):
    return 0.5 * x * (1.0 + lax.erf(x * (1.0 / math.sqrt(2.0))))


def _silu(x):
    return x * (1.0 / (1.0 + jnp.exp(-x)))


def _proj_kernel(x_ref, pos_ref, aux_ref, win_ref, gq_ref, wuq_ref, gkv_ref, wuk_ref,
                 wuv_ref, sg_ref, sb_ref, wsp_ref, bsp_ref,
                 q_out, k_out, v_out, ga_out, ob_out, cos_sc, sup_sc, sdn_sc):
    tm = x_ref.shape[1]
    x = x_ref[0].astype(_bf16)

    def proj(lo, hi):
        return _dot(x, win_ref[:, lo:hi])

    ang = pos_ref[0].astype(_f32) * aux_ref[0:1, :]
    cos8 = jnp.cos(ang)
    sin8 = jnp.sin(ang)
    lane = lax.broadcasted_iota(jnp.int32, cos8.shape, 1)
    in_lo = (lane >= ROPE_LO) & (lane < ROPE_LO + HALF)
    in_hi = (lane >= ROPE_LO + HALF) & (lane < ROPE_LO + ROPE)

    def place(t8, lane0, i):
        shift = (lane0 - HALF * i) % LANES
        return t8 if shift == 0 else pltpu.roll(t8, shift, 1)

    for i in range(POS_PER_ROW):
        rows = pl.ds(i, tm // POS_PER_ROW, stride=POS_PER_ROW)
        cos_sc[rows, :] = jnp.where(in_lo, place(cos8, ROPE_LO, i),
                                    jnp.where(in_hi, place(cos8, ROPE_LO + HALF, i), 1.0))
        sup_sc[rows, :] = jnp.where(in_hi, place(sin8, ROPE_LO + HALF, i), 0.0)
        sdn_sc[rows, :] = jnp.where(in_lo, -place(sin8, ROPE_LO, i), 0.0)

    def rope(t):
        return (t * cos_sc[...] + pltpu.roll(t, HALF, 1) * sup_sc[...]
                + pltpu.roll(t, LANES - HALF, 1) * sdn_sc[...])

    def rms(c, g_ref):
        return c * lax.rsqrt(jnp.mean(c * c, axis=-1, keepdims=True) + EPS) * g_ref[...]

    lat = proj(C_CQ, C_ZA)
    cqn = rms(lat[:, C_CQ:C_CKV], gq_ref).astype(_bf16)
    q = _dot(cqn, wuq_ref[...]) * Q_SCALE
    ckn = rms(lat[:, C_CKV:C_KR], gkv_ref).astype(_bf16)
    kn = _dot(ckn, wuk_ref[...])
    vv = _dot(ckn, wuv_ref[...])
    kr = rope(lat[:, C_KR:C_ZA])
    ones_col = aux_ref[1:2, :]
    for h in range(HEADS):
        sl = slice(h * LANES, (h + 1) * LANES)
        q_out[0, h] = rope(q[:, sl]).astype(_bf16)
        k_out[0, h] = (kn[:, sl] + kr).astype(_bf16)
        v_out[0, h] = (vv[:, sl] + ones_col).astype(_bf16)

    ga_out[0] = _silu(proj(C_ZA, C_U)).astype(_bf16)

    u = _gelu(proj(C_U, C_V))
    v = _gelu(proj(C_V, C_ZB))
    mu = jnp.mean(v, axis=-1, keepdims=True)
    vc = v - mu
    var = jnp.mean(vc * vc, axis=-1, keepdims=True)
    vn = (vc * lax.rsqrt(var + EPS) * sg_ref[...] + sb_ref[...])
    gb = _silu(proj(C_ZB, C_END))

    row = lax.broadcasted_iota(jnp.int32, (CHUNK, 2 * CHUNK), 0)
    col = lax.broadcasted_iota(jnp.int32, (CHUNK, 2 * CHUNK), 1)
    tri = (col % CHUNK) <= row
    lane = lax.broadcasted_iota(jnp.int32, (CHUNK, LANES), 1)
    lo_half = lane < G_HEAD_DIM
    for pair in range(G_HEADS // 2):
        w_pair = jnp.where(tri, wsp_ref[pair], 0.0).astype(_bf16)
        ls = slice(pair * LANES, (pair + 1) * LANES)
        for c in range(tm // CHUNK):
            rs = slice(c * CHUNK, (c + 1) * CHUNK)
            vp = vn[rs, ls]
            stacked = jnp.concatenate(
                [jnp.where(lo_half, vp, 0.0), jnp.where(lo_half, 0.0, vp)], axis=0
            ).astype(_bf16)
            sv = _dot(w_pair, stacked) + bsp_ref[:, ls]
            ob_out[0, rs, ls] = (u[rs, ls] * sv * gb[rs, ls]).astype(_bf16)


def _attn_kernel(q_ref, k_ref, v_ref, g_ref, o_ref, s_sc, m_sc, acc_sc):
    seq = q_ref.shape[2]
    nq = seq // TQ
    low = TK

    def scores(hh, slot, q0, j, r0=0):
        ks = pl.multiple_of(j * TK, TK)
        k = k_ref[0, hh, pl.ds(ks, TK), :]
        q = q_ref[0, hh, pl.ds(pl.multiple_of(q0 + r0, TK), TQ - r0), :]
        s_sc[hh, slot, r0:, :] = lax.dot_general(q, k, (((1,), (1,)), ((), ())),
                                                 preferred_element_type=_f32)

    def softmax_pv(hh, slot, j, r0=0, masked=False):
        ks = pl.multiple_of(j * TK, TK)
        s = s_sc[hh, slot, r0:, :]
        if masked:
            r = lax.broadcasted_iota(jnp.int32, s.shape, 0)
            c = lax.broadcasted_iota(jnp.int32, s.shape, 1)
            s = jnp.where(c <= r, s, NEG_BIG)
        m_prev = m_sc[hh, r0:, :]
        m_next = jnp.maximum(m_prev, jnp.max(s, axis=1, keepdims=True))
        p = jnp.exp2(s - jnp.tile(m_next, (1, TK // LANES))).astype(_bf16)
        alpha = jnp.exp2(m_prev - m_next)
        v = v_ref[0, hh, pl.ds(ks, TK), :]
        acc_sc[hh, r0:, :] = alpha * acc_sc[hh, r0:, :] + _dot(p, v)
        m_sc[hh, r0:, :] = m_next

    scores(0, 0, 0, 0)

    def q_tile(qi, carry):
        q0 = pl.multiple_of(qi * TQ, TQ)
        m_sc[...] = jnp.full(m_sc.shape, NEG_BIG, _f32)
        acc_sc[...] = jnp.zeros(acc_sc.shape, _f32)

        def two_blocks(i, c):
            j = 2 * i
            scores(1, 0, q0, j)
            scores(0, 1, q0, j + 1)
            softmax_pv(0, 0, j)
            softmax_pv(1, 0, j)
            scores(1, 1, q0, j + 1)
            scores(0, 0, q0, j + 2)
            softmax_pv(0, 1, j + 1)
            softmax_pv(1, 1, j + 1)
            return c

        lax.fori_loop(0, qi, two_blocks, 0)
        j0 = (TQ // TK) * qi
        scores(1, 0, q0, j0)
        scores(0, 1, q0, j0 + 1, low)
        softmax_pv(0, 0, j0, masked=True)
        softmax_pv(1, 0, j0, masked=True)
        scores(1, 1, q0, j0 + 1, low)
        scores(0, 0, jnp.minimum(qi + 1, nq - 1) * TQ, 0)
        softmax_pv(0, 1, j0 + 1, low, masked=True)
        softmax_pv(1, 1, j0 + 1, low, masked=True)

        outs = []
        for hh in range(HEADS_PER_STEP):
            acc = acc_sc[hh]
            outs.append(acc * (1.0 / acc[:, V_DIM:V_DIM + 1]))
        lane = lax.broadcasted_iota(jnp.int32, (TQ, LANES), 1)
        both = jnp.where(lane < V_DIM, outs[0], pltpu.roll(outs[1], V_DIM, 1))
        rows = pl.ds(q0, TQ)
        o_ref[0, rows, :] = (both * g_ref[0, rows, :].astype(_f32)).astype(_bf16)
        return carry

    lax.fori_loop(0, nq, q_tile, 0)


def _out_kernel(x_ref, oa_ref, ob_ref, wo_ref, g_ref, b_ref, o_ref):
    y = _dot(oa_ref[...], wo_ref[:A_WIDTH, :]) + _dot(ob_ref[...], wo_ref[A_WIDTH:, :])
    z = DN_ALPHA * x_ref[...] + y
    mu = jnp.mean(z, axis=-1, keepdims=True)
    zc = z - mu
    var = jnp.mean(zc * zc, axis=-1, keepdims=True)
    o_ref[...] = zc * lax.rsqrt(var + EPS) * g_ref[...] + b_ref[...]


def _prep_weights(w_in, w_uq, w_ukv, w_spatial, b_spatial):
    splits = np.cumsum([Q_LORA, KV_LORA, ROPE, A_WIDTH, G_WIDTH, G_WIDTH]).tolist()
    w_cq, w_ckv, w_kr, w_za, w_u, w_v = jnp.split(w_in, splits, axis=1)[:6]
    w_zb = w_in[:, splits[-1]:]
    kr_pad = jnp.pad(w_kr, ((0, 0), (ROPE_LO, LANES - ROPE_LO - ROPE)))
    win_p = jnp.concatenate([w_cq, w_ckv, kr_pad, w_za, w_u, w_v, w_zb], axis=1).astype(_bf16)

    wq = w_uq.reshape(Q_LORA, HEADS, NOPE + ROPE)
    wq = jnp.pad(wq, ((0, 0), (0, 0), (0, LANES - NOPE - ROPE))).reshape(Q_LORA, HP)
    wkv = w_ukv.reshape(KV_LORA, HEADS, NOPE + V_DIM)
    wk = jnp.pad(wkv[:, :, :NOPE], ((0, 0), (0, 0), (0, LANES - NOPE))).reshape(KV_LORA, HP)
    wv = jnp.pad(wkv[:, :, NOPE:], ((0, 0), (0, 0), (0, LANES - V_DIM))).reshape(KV_LORA, HP)

    wsp = w_spatial.reshape(G_HEADS // 2, 2, CHUNK, CHUNK).transpose(0, 2, 1, 3)
    wsp = wsp.reshape(G_HEADS // 2, CHUNK, 2 * CHUNK)
    bsp = jnp.repeat(b_spatial.T, G_HEAD_DIM, axis=1)
    return win_p, wq.astype(_bf16), wk.astype(_bf16), wv.astype(_bf16), wsp, bsp


def _aux_table():
    inv_freq = 1.0 / (ROPE_THETA ** (jnp.arange(HALF, dtype=_f32) / HALF))
    freq = jnp.tile(inv_freq, POS_PER_ROW)
    one = jnp.asarray((np.arange(LANES) == V_DIM).astype(np.float32))
    return jnp.concatenate([freq[None], one[None], jnp.zeros((6, LANES), _f32)], axis=0)


def _pack_positions(positions):
    b, s = positions.shape
    return jnp.repeat(positions.reshape(b, s // POS_PER_ROW, POS_PER_ROW), HALF, axis=-1)


def kernel(x, positions, w_in, q_norm_g, w_uq, kv_norm_g, w_ukv, sgu_norm_g, sgu_norm_b,
           w_spatial, b_spatial, w_out, ln_g, ln_b):
    b, s, d = x.shape
    assert d == D_MODEL and s % TQ == 0 and s % TM_PROJ == 0 and TQ == 2 * TK and HEADS_PER_STEP == 2
    win_p, wq, wk, wv, wsp, bsp = _prep_weights(w_in, w_uq, w_ukv, w_spatial, b_spatial)
    aux = _aux_table()
    pos8 = _pack_positions(positions)

    const2 = lambda shape: pl.BlockSpec(shape, lambda bi, i: (0, 0))
    const3 = lambda shape: pl.BlockSpec(shape, lambda bi, i: (0, 0, 0))
    head_spec = pl.BlockSpec((1, HEADS, TM_PROJ, LANES), lambda bi, i: (bi, 0, i, 0))
    row_spec = lambda w: pl.BlockSpec((1, TM_PROJ, w), lambda bi, i: (bi, i, 0))
    hshape = jax.ShapeDtypeStruct((b, HEADS, s, LANES), _bf16)
    qp, kp, vp, ga, ob = pl.pallas_call(
        _proj_kernel,
        grid=(b, s // TM_PROJ),
        in_specs=[row_spec(D_MODEL),
                  pl.BlockSpec((1, TM_PROJ // POS_PER_ROW, LANES), lambda bi, i: (bi, i, 0)),
                  const2((8, LANES)),
                  const2((D_MODEL, C_END)), const2((1, Q_LORA)), const2((Q_LORA, HP)),
                  const2((1, KV_LORA)), const2((KV_LORA, HP)), const2((KV_LORA, HP)),
                  const2((1, G_WIDTH)), const2((1, G_WIDTH)),
                  const3((G_HEADS // 2, CHUNK, 2 * CHUNK)), const2((CHUNK, G_WIDTH))],
        out_specs=[head_spec, head_spec, head_spec, row_spec(A_WIDTH), row_spec(G_WIDTH)],
        out_shape=[hshape, hshape, hshape,
                   jax.ShapeDtypeStruct((b, s, A_WIDTH), _bf16),
                   jax.ShapeDtypeStruct((b, s, G_WIDTH), _bf16)],
        scratch_shapes=[pltpu.VMEM((TM_PROJ, LANES), _f32)] * 3,
        compiler_params=pltpu.CompilerParams(
            dimension_semantics=("arbitrary", "arbitrary"), vmem_limit_bytes=VMEM_LIMIT),
        name="hyb_proj",
    )(x, pos8, aux, win_p, q_norm_g.reshape(1, -1), wq, kv_norm_g.reshape(1, -1), wk, wv,
      sgu_norm_g.reshape(1, -1), sgu_norm_b.reshape(1, -1), wsp, bsp)

    hps = HEADS_PER_STEP
    per_pair = pl.BlockSpec((1, hps, s, LANES), lambda bi, hp: (bi, hp, 0, 0))
    gate_spec = pl.BlockSpec((1, s, LANES), lambda bi, hp: (bi, 0, hp))
    oa = pl.pallas_call(
        _attn_kernel,
        grid=(b, HEADS // hps),
        in_specs=[per_pair, per_pair, per_pair, gate_spec],
        out_specs=gate_spec,
        out_shape=jax.ShapeDtypeStruct((b, s, A_WIDTH), _bf16),
        scratch_shapes=[pltpu.VMEM((hps, 2, TQ, TK), _f32), pltpu.VMEM((hps, TQ, LANES), _f32),
                        pltpu.VMEM((hps, TQ, LANES), _f32)],
        compiler_params=pltpu.CompilerParams(
            dimension_semantics=("arbitrary", "arbitrary"), vmem_limit_bytes=VMEM_LIMIT_ATTN),
        name="hyb_attn",
    )(qp, kp, vp, ga)

    r = b * s
    rows = lambda w: pl.BlockSpec((TM_OUT, w), lambda i: (i, 0))
    fixed = lambda shape: pl.BlockSpec(shape, lambda i: (0, 0))
    out = pl.pallas_call(
        _out_kernel,
        grid=(r // TM_OUT,),
        in_specs=[rows(D_MODEL), rows(A_WIDTH), rows(G_WIDTH), fixed((D_MODEL, D_MODEL)),
                  fixed((1, D_MODEL)), fixed((1, D_MODEL))],
        out_specs=rows(D_MODEL),
        out_shape=jax.ShapeDtypeStruct((r, D_MODEL), x.dtype),
        compiler_params=pltpu.CompilerParams(
            dimension_semantics=("arbitrary",), vmem_limit_bytes=VMEM_LIMIT),
        name="hyb_out",
    )(x.reshape(r, d), oa.reshape(r, A_WIDTH), ob.reshape(r, G_WIDTH), w_out.astype(_bf16),
      ln_g.reshape(1, -1), ln_b.reshape(1, -1))
    return out.reshape(b, s, d)
```

```python
import math

import jax
import jax.numpy as jnp
import numpy as np
from jax import lax
from jax.experimental import pallas as pl
from jax.experimental.pallas import tpu as pltpu

D_MODEL = 1024
HEADS = 8
NOPE = 64
ROPE = 32
HALF = ROPE // 2
POS_PER_ROW = 8
V_DIM = 64
A_WIDTH = HEADS * V_DIM
Q_LORA = 256
KV_LORA = 128
ROPE_THETA = 10000.0
CHUNK = 128
G_HEADS = 8
G_WIDTH = 512
G_HEAD_DIM = G_WIDTH // G_HEADS
DN_ALPHA = 2.0 ** 0.25
EPS = 1e-5
SM_SCALE = 1.0 / math.sqrt(NOPE + ROPE)
Q_SCALE = SM_SCALE * math.log2(math.e)

LANES = 128
HP = HEADS * LANES
ROPE_LO = NOPE

C_CQ = 0
C_CKV = C_CQ + Q_LORA
C_KR = C_CKV + KV_LORA
C_ZA = C_KR + LANES
C_U = C_ZA + A_WIDTH
C_V = C_U + G_WIDTH
C_ZB = C_V + G_WIDTH
C_END = C_ZB + G_WIDTH

TM_PROJ = 512
TM_OUT = 1024
TQ = 1024
TK = 512
HEADS_PER_STEP = 2
NEG_BIG = -1e30

VMEM_LIMIT = 48 * 1024 * 1024
VMEM_LIMIT_ATTN = 52 * 1024 * 1024

_f32 = jnp.float32
_bf16 = jnp.bfloat16


def _dot(a, b):
    return jnp.dot(a, b, preferred_element_type=_f32)


def _dot_nt(a, b):
    return lax.dot_general(a, b, (((1,), (1,)), ((), ())), preferred_element_type=_f32)


def _gelu(x):
    return 0.5 * x * (1.0 + lax.erf(x * (1.0 / math.sqrt(2.0))))


def _silu(x):
    return x * (1.0 / (1.0 + jnp.exp(-x)))


def _proj_kernel(x_ref, pos_ref, aux_ref, win_ref, gq_ref, wuq_ref, gkv_ref, wuk_ref,
                 wuvt_ref, sg_ref, sb_ref, wsp_ref, bsp_ref,
                 q_out, k_out, vt_out, ga_out, ob_out, cos_sc, sup_sc, sdn_sc):
    tm = x_ref.shape[1]
    x = x_ref[0].astype(_bf16)

    def proj(lo, hi):
        return _dot(x, win_ref[:, lo:hi])

    ang = pos_ref[0].astype(_f32) * aux_ref[0:1, :]
    cos8 = jnp.cos(ang)
    sin8 = jnp.sin(ang)
    lane = lax.broadcasted_iota(jnp.int32, cos8.shape, 1)
    in_lo = (lane >= ROPE_LO) & (lane < ROPE_LO + HALF)
    in_hi = (lane >= ROPE_LO + HALF) & (lane < ROPE_LO + ROPE)

    def place(t8, lane0, i):
        shift = (lane0 - HALF * i) % LANES
        return t8 if shift == 0 else pltpu.roll(t8, shift, 1)

    for i in range(POS_PER_ROW):
        rows = pl.ds(i, tm // POS_PER_ROW, stride=POS_PER_ROW)
        cos_sc[rows, :] = jnp.where(in_lo, place(cos8, ROPE_LO, i),
                                    jnp.where(in_hi, place(cos8, ROPE_LO + HALF, i), 1.0))
        sup_sc[rows, :] = jnp.where(in_hi, place(sin8, ROPE_LO + HALF, i), 0.0)
        sdn_sc[rows, :] = jnp.where(in_lo, -place(sin8, ROPE_LO, i), 0.0)

    def rope(t):
        return (t * cos_sc[...] + pltpu.roll(t, HALF, 1) * sup_sc[...]
                + pltpu.roll(t, LANES - HALF, 1) * sdn_sc[...])

    def rms(c, g_ref):
        return c * lax.rsqrt(jnp.mean(c * c, axis=-1, keepdims=True) + EPS) * g_ref[...]

    lat = proj(C_CQ, C_ZA)
    cqn = rms(lat[:, C_CQ:C_CKV], gq_ref).astype(_bf16)
    q = _dot(cqn, wuq_ref[...]) * Q_SCALE
    ckn = rms(lat[:, C_CKV:C_KR], gkv_ref).astype(_bf16)
    kn = _dot(ckn, wuk_ref[...])
    vt = _dot_nt(wuvt_ref[...], ckn)
    kr = rope(lat[:, C_KR:C_ZA])
    ones_row = lax.broadcasted_iota(jnp.int32, (LANES, tm), 0) == V_DIM
    for h in range(HEADS):
        sl = slice(h * LANES, (h + 1) * LANES)
        q_out[0, h] = rope(q[:, sl]).astype(_bf16)
        k_out[0, h] = (kn[:, sl] + kr).astype(_bf16)
        vt_out[0, h] = jnp.where(ones_row, 1.0, vt[sl, :]).astype(_bf16)

    ga_out[0] = _silu(proj(C_ZA, C_U)).astype(_bf16)

    u = _gelu(proj(C_U, C_V))
    v = _gelu(proj(C_V, C_ZB))
    mu = jnp.mean(v, axis=-1, keepdims=True)
    vc = v - mu
    var = jnp.mean(vc * vc, axis=-1, keepdims=True)
    vn = (vc * lax.rsqrt(var + EPS) * sg_ref[...] + sb_ref[...])
    gb = _silu(proj(C_ZB, C_END))

    row = lax.broadcasted_iota(jnp.int32, (CHUNK, 2 * CHUNK), 0)
    col = lax.broadcasted_iota(jnp.int32, (CHUNK, 2 * CHUNK), 1)
    tri = (col % CHUNK) <= row
    lane = lax.broadcasted_iota(jnp.int32, (CHUNK, LANES), 1)
    lo_half = lane < G_HEAD_DIM
    for pair in range(G_HEADS // 2):
        w_pair = jnp.where(tri, wsp_ref[pair], 0.0).astype(_bf16)
        ls = slice(pair * LANES, (pair + 1) * LANES)
        for c in range(tm // CHUNK):
            rs = slice(c * CHUNK, (c + 1) * CHUNK)
            vp = vn[rs, ls]
            stacked = jnp.concatenate(
                [jnp.where(lo_half, vp, 0.0), jnp.where(lo_half, 0.0, vp)], axis=0
            ).astype(_bf16)
            sv = _dot(w_pair, stacked) + bsp_ref[:, ls]
            ob_out[0, rs, ls] = (u[rs, ls] * sv * gb[rs, ls]).astype(_bf16)


def _attn_kernel(q_ref, k_ref, vt_ref, g_ref, o_ref, s_sc, m_sc, acc_sc):
    seq = q_ref.shape[2]
    nq = seq // TQ
    low = TK

    def scores(hh, slot, q0, j, c0=0):
        ks = pl.multiple_of(j * TK, TK)
        k = k_ref[0, hh, pl.ds(ks, TK), :]
        q = q_ref[0, hh, pl.ds(pl.multiple_of(q0 + c0, TK), TQ - c0), :]
        s_sc[hh, slot, :, c0:] = _dot_nt(k, q)

    def softmax_pv(hh, slot, j, c0=0, masked=False):
        ks = pl.multiple_of(j * TK, TK)
        s = s_sc[hh, slot, :, c0:]
        if masked:
            r = lax.broadcasted_iota(jnp.int32, s.shape, 0)
            c = lax.broadcasted_iota(jnp.int32, s.shape, 1)
            s = jnp.where(r <= c, s, NEG_BIG)
        m_prev = m_sc[hh, :, c0:]
        m_next = jnp.maximum(m_prev, jnp.max(s, axis=0, keepdims=True))
        p = jnp.exp2(s - m_next).astype(_bf16)
        alpha = jnp.exp2(m_prev - m_next)
        vt = vt_ref[0, hh, :, pl.ds(ks, TK)]
        acc_sc[hh, :, c0:] = alpha * acc_sc[hh, :, c0:] + _dot(vt, p)
        m_sc[hh, :, c0:] = m_next

    scores(0, 0, 0, 0)

    def q_tile(qi, carry):
        q0 = pl.multiple_of(qi * TQ, TQ)
        m_sc[...] = jnp.full(m_sc.shape, NEG_BIG, _f32)
        acc_sc[...] = jnp.zeros(acc_sc.shape, _f32)

        def two_blocks(i, c):
            j = 2 * i
            scores(1, 0, q0, j)
            softmax_pv(0, 0, j)
            scores(0, 1, q0, j + 1)
            softmax_pv(1, 0, j)
            scores(1, 1, q0, j + 1)
            softmax_pv(0, 1, j + 1)
            scores(0, 0, q0, j + 2)
            softmax_pv(1, 1, j + 1)
            return c

        lax.fori_loop(0, qi, two_blocks, 0)
        j0 = (TQ // TK) * qi
        scores(1, 0, q0, j0)
        softmax_pv(0, 0, j0, masked=True)
        scores(0, 1, q0, j0 + 1, low)
        softmax_pv(1, 0, j0, masked=True)
        scores(1, 1, q0, j0 + 1, low)
        softmax_pv(0, 1, j0 + 1, low, masked=True)
        scores(0, 0, jnp.minimum(qi + 1, nq - 1) * TQ, 0)
        softmax_pv(1, 1, j0 + 1, low, masked=True)

        outs = []
        for hh in range(HEADS_PER_STEP):
            acc = acc_sc[hh]
            outs.append((acc * (1.0 / acc[V_DIM:V_DIM + 1, :])).T)
        lane = lax.broadcasted_iota(jnp.int32, (TQ, LANES), 1)
        both = jnp.where(lane < V_DIM, outs[0], pltpu.roll(outs[1], V_DIM, 1))
        rows = pl.ds(q0, TQ)
        o_ref[0, rows, :] = (both * g_ref[0, rows, :].astype(_f32)).astype(_bf16)
        return carry

    lax.fori_loop(0, nq, q_tile, 0)


def _out_kernel(x_ref, oa_ref, ob_ref, wo_ref, g_ref, b_ref, o_ref):
    y = _dot(oa_ref[...], wo_ref[:A_WIDTH, :]) + _dot(ob_ref[...], wo_ref[A_WIDTH:, :])
    z = DN_ALPHA * x_ref[...] + y
    mu = jnp.mean(z, axis=-1, keepdims=True)
    zc = z - mu
    var = jnp.mean(zc * zc, axis=-1, keepdims=True)
    o_ref[...] = zc * lax.rsqrt(var + EPS) * g_ref[...] + b_ref[...]


def _prep_weights(w_in, w_uq, w_ukv, w_spatial, b_spatial):
    splits = np.cumsum([Q_LORA, KV_LORA, ROPE, A_WIDTH, G_WIDTH, G_WIDTH]).tolist()
    w_cq, w_ckv, w_kr, w_za, w_u, w_v = jnp.split(w_in, splits, axis=1)[:6]
    w_zb = w_in[:, splits[-1]:]
    kr_pad = jnp.pad(w_kr, ((0, 0), (ROPE_LO, LANES - ROPE_LO - ROPE)))
    win_p = jnp.concatenate([w_cq, w_ckv, kr_pad, w_za, w_u, w_v, w_zb], axis=1).astype(_bf16)

    wq = w_uq.reshape(Q_LORA, HEADS, NOPE + ROPE)
    wq = jnp.pad(wq, ((0, 0), (0, 0), (0, LANES - NOPE - ROPE))).reshape(Q_LORA, HP)
    wkv = w_ukv.reshape(KV_LORA, HEADS, NOPE + V_DIM)
    wk = jnp.pad(wkv[:, :, :NOPE], ((0, 0), (0, 0), (0, LANES - NOPE))).reshape(KV_LORA, HP)
    wvt = jnp.pad(wkv[:, :, NOPE:], ((0, 0), (0, 0), (0, LANES - V_DIM))).reshape(KV_LORA, HP).T

    wsp = w_spatial.reshape(G_HEADS // 2, 2, CHUNK, CHUNK).transpose(0, 2, 1, 3)
    wsp = wsp.reshape(G_HEADS // 2, CHUNK, 2 * CHUNK)
    bsp = jnp.repeat(b_spatial.T, G_HEAD_DIM, axis=1)
    return win_p, wq.astype(_bf16), wk.astype(_bf16), wvt.astype(_bf16), wsp, bsp


def _aux_table():
    inv_freq = 1.0 / (ROPE_THETA ** (jnp.arange(HALF, dtype=_f32) / HALF))
    freq = jnp.tile(inv_freq, POS_PER_ROW)
    return jnp.concatenate([freq[None], jnp.zeros((7, LANES), _f32)], axis=0)


def _pack_positions(positions):
    b, s = positions.shape
    return jnp.repeat(positions.reshape(b, s // POS_PER_ROW, POS_PER_ROW), HALF, axis=-1)


def kernel(x, positions, w_in, q_norm_g, w_uq, kv_norm_g, w_ukv, sgu_norm_g, sgu_norm_b,
           w_spatial, b_spatial, w_out, ln_g, ln_b):
    b, s, d = x.shape
    assert d == D_MODEL and s % TQ == 0 and s % TM_PROJ == 0 and TQ == 2 * TK and HEADS_PER_STEP == 2
    win_p, wq, wk, wvt, wsp, bsp = _prep_weights(w_in, w_uq, w_ukv, w_spatial, b_spatial)
    aux = _aux_table()
    pos8 = _pack_positions(positions)

    const2 = lambda shape: pl.BlockSpec(shape, lambda bi, i: (0, 0))
    const3 = lambda shape: pl.BlockSpec(shape, lambda bi, i: (0, 0, 0))
    head_spec = pl.BlockSpec((1, HEADS, TM_PROJ, LANES), lambda bi, i: (bi, 0, i, 0))
    row_spec = lambda w: pl.BlockSpec((1, TM_PROJ, w), lambda bi, i: (bi, i, 0))
    hshape = jax.ShapeDtypeStruct((b, HEADS, s, LANES), _bf16)
    qp, kp, vtp, ga, ob = pl.pallas_call(
        _proj_kernel,
        grid=(b, s // TM_PROJ),
        in_specs=[row_spec(D_MODEL),
                  pl.BlockSpec((1, TM_PROJ // POS_PER_ROW, LANES), lambda bi, i: (bi, i, 0)),
                  const2((8, LANES)),
                  const2((D_MODEL, C_END)), const2((1, Q_LORA)), const2((Q_LORA, HP)),
                  const2((1, KV_LORA)), const2((KV_LORA, HP)), const2((HP, KV_LORA)),
                  const2((1, G_WIDTH)), const2((1, G_WIDTH)),
                  const3((G_HEADS // 2, CHUNK, 2 * CHUNK)), const2((CHUNK, G_WIDTH))],
        out_specs=[head_spec, head_spec,
                   pl.BlockSpec((1, HEADS, LANES, TM_PROJ), lambda bi, i: (bi, 0, 0, i)),
                   row_spec(A_WIDTH), row_spec(G_WIDTH)],
        out_shape=[hshape, hshape, jax.ShapeDtypeStruct((b, HEADS, LANES, s), _bf16),
                   jax.ShapeDtypeStruct((b, s, A_WIDTH), _bf16),
                   jax.ShapeDtypeStruct((b, s, G_WIDTH), _bf16)],
        scratch_shapes=[pltpu.VMEM((TM_PROJ, LANES), _f32)] * 3,
        compiler_params=pltpu.CompilerParams(
            dimension_semantics=("arbitrary", "arbitrary"), vmem_limit_bytes=VMEM_LIMIT),
        name="hyb_proj",
    )(x, pos8, aux, win_p, q_norm_g.reshape(1, -1), wq, kv_norm_g.reshape(1, -1), wk, wvt,
      sgu_norm_g.reshape(1, -1), sgu_norm_b.reshape(1, -1), wsp, bsp)

    hps = HEADS_PER_STEP
    per_pair = pl.BlockSpec((1, hps, s, LANES), lambda bi, hp: (bi, hp, 0, 0))
    gate_spec = pl.BlockSpec((1, s, LANES), lambda bi, hp: (bi, 0, hp))
    oa = pl.pallas_call(
        _attn_kernel,
        grid=(b, HEADS // hps),
        in_specs=[per_pair, per_pair,
                  pl.BlockSpec((1, hps, LANES, s), lambda bi, hp: (bi, hp, 0, 0)), gate_spec],
        out_specs=gate_spec,
        out_shape=jax.ShapeDtypeStruct((b, s, A_WIDTH), _bf16),
        scratch_shapes=[pltpu.VMEM((hps, 2, TK, TQ), _f32), pltpu.VMEM((hps, 1, TQ), _f32),
                        pltpu.VMEM((hps, LANES, TQ), _f32)],
        compiler_params=pltpu.CompilerParams(
            dimension_semantics=("arbitrary", "arbitrary"), vmem_limit_bytes=VMEM_LIMIT_ATTN),
        name="hyb_attn",
    )(qp, kp, vtp, ga)

    r = b * s
    rows = lambda w: pl.BlockSpec((TM_OUT, w), lambda i: (i, 0))
    fixed = lambda shape: pl.BlockSpec(shape, lambda i: (0, 0))
    out = pl.pallas_call(
        _out_kernel,
        grid=(r // TM_OUT,),
        in_specs=[rows(D_MODEL), rows(A_WIDTH), rows(G_WIDTH), fixed((D_MODEL, D_MODEL)),
                  fixed((1, D_MODEL)), fixed((1, D_MODEL))],
        out_specs=rows(D_MODEL),
        out_shape=jax.ShapeDtypeStruct((r, D_MODEL), x.dtype),
        compiler_params=pltpu.CompilerParams(
            dimension_semantics=("arbitrary",), vmem_limit_bytes=VMEM_LIMIT),
        name="hyb_out",
    )(x.reshape(r, d), oa.reshape(r, A_WIDTH), ob.reshape(r, G_WIDTH), w_out.astype(_bf16),
      ln_g.reshape(1, -1), ln_b.reshape(1, -1))
    return out.reshape(b, s, d)
```

```python
import math

import jax
import jax.numpy as jnp
import numpy as np
from jax import lax
from jax.experimental import pallas as pl
from jax.experimental.pallas import tpu as pltpu

D_MODEL = 1024
HEADS = 8
NOPE = 64
ROPE = 32
HALF = ROPE // 2
POS_PER_ROW = 8
V_DIM = 64
A_WIDTH = HEADS * V_DIM
Q_LORA = 256
KV_LORA = 128
ROPE_THETA = 10000.0
CHUNK = 128
G_HEADS = 8
G_WIDTH = 512
G_HEAD_DIM = G_WIDTH // G_HEADS
DN_ALPHA = 2.0 ** 0.25
EPS = 1e-5
SM_SCALE = 1.0 / math.sqrt(NOPE + ROPE)
Q_SCALE = SM_SCALE * math.log2(math.e)

LANES = 128
HP = HEADS * LANES
ROPE_LO = NOPE

C_CQ = 0
C_CKV = C_CQ + Q_LORA
C_KR = C_CKV + KV_LORA
C_ZA = C_KR + LANES
C_U = C_ZA + A_WIDTH
C_V = C_U + G_WIDTH
C_ZB = C_V + G_WIDTH
C_END = C_ZB + G_WIDTH

TM_PROJ = 512
PROJ_CHUNK = 512
TM_OUT = 1024
OUT_CHUNK = 256
TQ = 1024
TK = 512
HEADS_PER_STEP = 2
NEG_BIG = -1e30

VMEM_LIMIT = 48 * 1024 * 1024
VMEM_LIMIT_ATTN = 52 * 1024 * 1024

_f32 = jnp.float32
_bf16 = jnp.bfloat16


def _dot(a, b):
    return jnp.dot(a, b, preferred_element_type=_f32)


def _dot_nt(a, b):
    return lax.dot_general(a, b, (((1,), (1,)), ((), ())), preferred_element_type=_f32)


def _gelu(x):
    return 0.5 * x * (1.0 + lax.erf(x * (1.0 / math.sqrt(2.0))))


def _silu(x):
    return x * (1.0 / (1.0 + jnp.exp(-x)))


def _proj_kernel(x_ref, pos_ref, aux_ref, win_ref, gq_ref, wuq_ref, gkv_ref, wuk_ref,
                 wuvt_ref, sg_ref, sb_ref, wsp_ref, bsp_ref,
                 q_out, k_out, vt_out, ga_out, ob_out, cos_sc, sup_sc, sdn_sc):
    tm = x_ref.shape[1]

    ang = pos_ref[0].astype(_f32) * aux_ref[0:1, :]
    cos8 = jnp.cos(ang)
    sin8 = jnp.sin(ang)
    lane = lax.broadcasted_iota(jnp.int32, cos8.shape, 1)
    in_lo = (lane >= ROPE_LO) & (lane < ROPE_LO + HALF)
    in_hi = (lane >= ROPE_LO + HALF) & (lane < ROPE_LO + ROPE)

    def place(t8, lane0, i):
        shift = (lane0 - HALF * i) % LANES
        return t8 if shift == 0 else pltpu.roll(t8, shift, 1)

    for i in range(POS_PER_ROW):
        rows = pl.ds(i, tm // POS_PER_ROW, stride=POS_PER_ROW)
        cos_sc[rows, :] = jnp.where(in_lo, place(cos8, ROPE_LO, i),
                                    jnp.where(in_hi, place(cos8, ROPE_LO + HALF, i), 1.0))
        sup_sc[rows, :] = jnp.where(in_hi, place(sin8, ROPE_LO + HALF, i), 0.0)
        sdn_sc[rows, :] = jnp.where(in_lo, -place(sin8, ROPE_LO, i), 0.0)

    def rms(c, g_ref):
        return c * lax.rsqrt(jnp.mean(c * c, axis=-1, keepdims=True) + EPS) * g_ref[...]

    row = lax.broadcasted_iota(jnp.int32, (CHUNK, 2 * CHUNK), 0)
    col = lax.broadcasted_iota(jnp.int32, (CHUNK, 2 * CHUNK), 1)
    tri = (col % CHUNK) <= row
    lo_half = lax.broadcasted_iota(jnp.int32, (CHUNK, LANES), 1) < G_HEAD_DIM
    ones_row = lax.broadcasted_iota(jnp.int32, (LANES, PROJ_CHUNK), 0) == V_DIM

    for rc in range(tm // PROJ_CHUNK):
        rs = slice(rc * PROJ_CHUNK, (rc + 1) * PROJ_CHUNK)
        x = x_ref[0, rs, :].astype(_bf16)

        def proj(lo, hi, x=x):
            return _dot(x, win_ref[:, lo:hi])

        def rope(t, rs=rs):
            return (t * cos_sc[rs, :] + pltpu.roll(t, HALF, 1) * sup_sc[rs, :]
                    + pltpu.roll(t, LANES - HALF, 1) * sdn_sc[rs, :])

        lat = proj(C_CQ, C_ZA)
        cqn = rms(lat[:, C_CQ:C_CKV], gq_ref).astype(_bf16)
        q = _dot(cqn, wuq_ref[...]) * Q_SCALE
        ckn = rms(lat[:, C_CKV:C_KR], gkv_ref).astype(_bf16)
        kn = _dot(ckn, wuk_ref[...])
        vt = _dot_nt(wuvt_ref[...], ckn)
        kr = rope(lat[:, C_KR:C_ZA])
        for h in range(HEADS):
            sl = slice(h * LANES, (h + 1) * LANES)
            q_out[0, h, rs, :] = rope(q[:, sl]).astype(_bf16)
            k_out[0, h, rs, :] = (kn[:, sl] + kr).astype(_bf16)
            vt_out[0, h, :, rs] = jnp.where(ones_row, 1.0, vt[sl, :]).astype(_bf16)

        ga_out[0, rs, :] = _silu(proj(C_ZA, C_U)).astype(_bf16)

        u = _gelu(proj(C_U, C_V))
        v = _gelu(proj(C_V, C_ZB))
        mu = jnp.mean(v, axis=-1, keepdims=True)
        vc = v - mu
        var = jnp.mean(vc * vc, axis=-1, keepdims=True)
        vn = (vc * lax.rsqrt(var + EPS) * sg_ref[...] + sb_ref[...])
        gb = _silu(proj(C_ZB, C_END))

        for pair in range(G_HEADS // 2):
            w_pair = jnp.where(tri, wsp_ref[pair], 0.0).astype(_bf16)
            ls = slice(pair * LANES, (pair + 1) * LANES)
            for c in range(PROJ_CHUNK // CHUNK):
                cs = slice(c * CHUNK, (c + 1) * CHUNK)
                os = slice(rc * PROJ_CHUNK + c * CHUNK, rc * PROJ_CHUNK + (c + 1) * CHUNK)
                vp = vn[cs, ls]
                stacked = jnp.concatenate(
                    [jnp.where(lo_half, vp, 0.0), jnp.where(lo_half, 0.0, vp)], axis=0
                ).astype(_bf16)
                sv = _dot(w_pair, stacked) + bsp_ref[:, ls]
                ob_out[0, os, ls] = (u[cs, ls] * sv * gb[cs, ls]).astype(_bf16)


def _attn_kernel(q_ref, k_ref, vt_ref, g_ref, o_ref, s_sc, m_sc, acc_sc):
    seq = q_ref.shape[2]
    nq = seq // TQ
    low = TK

    def scores(hh, slot, q0, j, c0=0):
        ks = pl.multiple_of(j * TK, TK)
        k = k_ref[0, hh, pl.ds(ks, TK), :]
        q = q_ref[0, hh, pl.ds(pl.multiple_of(q0 + c0, TK), TQ - c0), :]
        s_sc[hh, slot, :, c0:] = _dot_nt(k, q)

    def softmax_pv(hh, slot, j, c0=0, masked=False):
        ks = pl.multiple_of(j * TK, TK)
        s = s_sc[hh, slot, :, c0:]
        if masked:
            r = lax.broadcasted_iota(jnp.int32, s.shape, 0)
            c = lax.broadcasted_iota(jnp.int32, s.shape, 1)
            s = jnp.where(r <= c, s, NEG_BIG)
        m_prev = m_sc[hh, :, c0:]
        m_next = jnp.maximum(m_prev, jnp.max(s, axis=0, keepdims=True))
        p = jnp.exp2(s - m_next).astype(_bf16)
        alpha = jnp.exp2(m_prev - m_next)
        vt = vt_ref[0, hh, :, pl.ds(ks, TK)]
        acc_sc[hh, :, c0:] = alpha * acc_sc[hh, :, c0:] + _dot(vt, p)
        m_sc[hh, :, c0:] = m_next

    scores(0, 0, 0, 0)

    def q_tile(qi, carry):
        q0 = pl.multiple_of(qi * TQ, TQ)
        m_sc[...] = jnp.full(m_sc.shape, NEG_BIG, _f32)
        acc_sc[...] = jnp.zeros(acc_sc.shape, _f32)

        def two_blocks(j):
            scores(1, 0, q0, j)
            softmax_pv(0, 0, j)
            scores(0, 1, q0, j + 1)
            softmax_pv(1, 0, j)
            scores(1, 1, q0, j + 1)
            softmax_pv(0, 1, j + 1)
            scores(0, 0, q0, j + 2)
            softmax_pv(1, 1, j + 1)

        def four_blocks(i, c):
            two_blocks(4 * i)
            two_blocks(4 * i + 2)
            return c

        def last_two_blocks(i, c):
            two_blocks(4 * (qi // 2))
            return c

        lax.fori_loop(0, qi // 2, four_blocks, 0)
        lax.fori_loop(0, qi % 2, last_two_blocks, 0)
        j0 = (TQ // TK) * qi
        scores(1, 0, q0, j0)
        softmax_pv(0, 0, j0, masked=True)
        scores(0, 1, q0, j0 + 1, low)
        softmax_pv(1, 0, j0, masked=True)
        scores(1, 1, q0, j0 + 1, low)
        softmax_pv(0, 1, j0 + 1, low, masked=True)
        scores(0, 0, jnp.minimum(qi + 1, nq - 1) * TQ, 0)
        softmax_pv(1, 1, j0 + 1, low, masked=True)

        outs = []
        for hh in range(HEADS_PER_STEP):
            acc = acc_sc[hh]
            outs.append((acc * (1.0 / acc[V_DIM:V_DIM + 1, :])).T)
        lane = lax.broadcasted_iota(jnp.int32, (TQ, LANES), 1)
        both = jnp.where(lane < V_DIM, outs[0], pltpu.roll(outs[1], V_DIM, 1))
        rows = pl.ds(q0, TQ)
        o_ref[0, rows, :] = (both * g_ref[0, rows, :].astype(_f32)).astype(_bf16)
        return carry

    lax.fori_loop(0, nq, q_tile, 0)


def _out_kernel(x_ref, oa_ref, ob_ref, wo_ref, g_ref, b_ref, o_ref):
    for c in range(TM_OUT // OUT_CHUNK):
        rs = slice(c * OUT_CHUNK, (c + 1) * OUT_CHUNK)
        y = _dot(oa_ref[rs, :], wo_ref[:A_WIDTH, :]) + _dot(ob_ref[rs, :], wo_ref[A_WIDTH:, :])
        z = DN_ALPHA * x_ref[rs, :] + y
        mu = jnp.mean(z, axis=-1, keepdims=True)
        zc = z - mu
        var = jnp.mean(zc * zc, axis=-1, keepdims=True)
        o_ref[rs, :] = zc * lax.rsqrt(var + EPS) * g_ref[...] + b_ref[...]


def _prep_weights(w_in, w_uq, w_ukv, w_spatial, b_spatial):
    kr0 = Q_LORA + KV_LORA
    zeros = lambda n: jnp.zeros((D_MODEL, n), w_in.dtype)
    win_p = jnp.concatenate([w_in[:, :kr0], zeros(ROPE_LO), w_in[:, kr0:kr0 + ROPE],
                             zeros(LANES - ROPE_LO - ROPE), w_in[:, kr0 + ROPE:]],
                            axis=1).astype(_bf16)

    wq = w_uq.reshape(Q_LORA, HEADS, NOPE + ROPE)
    wq = jnp.pad(wq, ((0, 0), (0, 0), (0, LANES - NOPE - ROPE))).reshape(Q_LORA, HP)
    wkv = w_ukv.reshape(KV_LORA, HEADS, NOPE + V_DIM)
    wk = jnp.pad(wkv[:, :, :NOPE], ((0, 0), (0, 0), (0, LANES - NOPE))).reshape(KV_LORA, HP)
    wvt = jnp.pad(wkv[:, :, NOPE:], ((0, 0), (0, 0), (0, LANES - V_DIM))).reshape(KV_LORA, HP).T

    wsp = w_spatial.reshape(G_HEADS // 2, 2, CHUNK, CHUNK).transpose(0, 2, 1, 3)
    wsp = wsp.reshape(G_HEADS // 2, CHUNK, 2 * CHUNK)
    bsp = jnp.repeat(b_spatial.T, G_HEAD_DIM, axis=1)
    return win_p, wq.astype(_bf16), wk.astype(_bf16), wvt.astype(_bf16), wsp, bsp


def _aux_table():
    inv_freq = 1.0 / (ROPE_THETA ** (jnp.arange(HALF, dtype=_f32) / HALF))
    freq = jnp.tile(inv_freq, POS_PER_ROW)
    return jnp.concatenate([freq[None], jnp.zeros((7, LANES), _f32)], axis=0)


def _pack_positions(positions):
    b, s = positions.shape
    return jnp.repeat(positions.reshape(b, s // POS_PER_ROW, POS_PER_ROW), HALF, axis=-1)


def kernel(x, positions, w_in, q_norm_g, w_uq, kv_norm_g, w_ukv, sgu_norm_g, sgu_norm_b,
           w_spatial, b_spatial, w_out, ln_g, ln_b):
    b, s, d = x.shape
    assert d == D_MODEL and s % TQ == 0 and s % TM_PROJ == 0 and TQ == 2 * TK and HEADS_PER_STEP == 2
    win_p, wq, wk, wvt, wsp, bsp = _prep_weights(w_in, w_uq, w_ukv, w_spatial, b_spatial)
    aux = _aux_table()
    pos8 = _pack_positions(positions)

    const2 = lambda shape: pl.BlockSpec(shape, lambda bi, i: (0, 0))
    const3 = lambda shape: pl.BlockSpec(shape, lambda bi, i: (0, 0, 0))
    head_spec = pl.BlockSpec((1, HEADS, TM_PROJ, LANES), lambda bi, i: (bi, 0, i, 0))
    row_spec = lambda w: pl.BlockSpec((1, TM_PROJ, w), lambda bi, i: (bi, i, 0))
    hshape = jax.ShapeDtypeStruct((b, HEADS, s, LANES), _bf16)
    qp, kp, vtp, ga, ob = pl.pallas_call(
        _proj_kernel,
        grid=(b, s // TM_PROJ),
        in_specs=[row_spec(D_MODEL),
                  pl.BlockSpec((1, TM_PROJ // POS_PER_ROW, LANES), lambda bi, i: (bi, i, 0)),
                  const2((8, LANES)),
                  const2((D_MODEL, C_END)), const2((1, Q_LORA)), const2((Q_LORA, HP)),
                  const2((1, KV_LORA)), const2((KV_LORA, HP)), const2((HP, KV_LORA)),
                  const2((1, G_WIDTH)), const2((1, G_WIDTH)),
                  const3((G_HEADS // 2, CHUNK, 2 * CHUNK)), const2((CHUNK, G_WIDTH))],
        out_specs=[head_spec, head_spec,
                   pl.BlockSpec((1, HEADS, LANES, TM_PROJ), lambda bi, i: (bi, 0, 0, i)),
                   row_spec(A_WIDTH), row_spec(G_WIDTH)],
        out_shape=[hshape, hshape, jax.ShapeDtypeStruct((b, HEADS, LANES, s), _bf16),
                   jax.ShapeDtypeStruct((b, s, A_WIDTH), _bf16),
                   jax.ShapeDtypeStruct((b, s, G_WIDTH), _bf16)],
        scratch_shapes=[pltpu.VMEM((TM_PROJ, LANES), _f32)] * 3,
        compiler_params=pltpu.CompilerParams(
            dimension_semantics=("arbitrary", "arbitrary"), vmem_limit_bytes=VMEM_LIMIT),
        name="hyb_proj",
    )(x, pos8, aux, win_p, q_norm_g.reshape(1, -1), wq, kv_norm_g.reshape(1, -1), wk, wvt,
      sgu_norm_g.reshape(1, -1), sgu_norm_b.reshape(1, -1), wsp, bsp)

    hps = HEADS_PER_STEP
    per_pair = pl.BlockSpec((1, hps, s, LANES), lambda bi, hp: (bi, hp, 0, 0))
    gate_spec = pl.BlockSpec((1, s, LANES), lambda bi, hp: (bi, 0, hp))
    oa = pl.pallas_call(
        _attn_kernel,
        grid=(b, HEADS // hps),
        in_specs=[per_pair, per_pair,
                  pl.BlockSpec((1, hps, LANES, s), lambda bi, hp: (bi, hp, 0, 0)), gate_spec],
        out_specs=gate_spec,
        out_shape=jax.ShapeDtypeStruct((b, s, A_WIDTH), _bf16),
        scratch_shapes=[pltpu.VMEM((hps, 2, TK, TQ), _f32), pltpu.VMEM((hps, 1, TQ), _f32),
                        pltpu.VMEM((hps, LANES, TQ), _f32)],
        compiler_params=pltpu.CompilerParams(
            dimension_semantics=("arbitrary", "arbitrary"), vmem_limit_bytes=VMEM_LIMIT_ATTN),
        name="hyb_attn",
    )(qp, kp, vtp, ga)

    r = b * s
    rows = lambda w: pl.BlockSpec((TM_OUT, w), lambda i: (i, 0))
    fixed = lambda shape: pl.BlockSpec(shape, lambda i: (0, 0))
    out = pl.pallas_call(
        _out_kernel,
        grid=(r // TM_OUT,),
        in_specs=[rows(D_MODEL), rows(A_WIDTH), rows(G_WIDTH), fixed((D_MODEL, D_MODEL)),
                  fixed((1, D_MODEL)), fixed((1, D_MODEL))],
        out_specs=rows(D_MODEL),
        out_shape=jax.ShapeDtypeStruct((r, D_MODEL), x.dtype),
        compiler_params=pltpu.CompilerParams(
            dimension_semantics=("arbitrary",), vmem_limit_bytes=VMEM_LIMIT),
        name="hyb_out",
    )(x.reshape(r, d), oa.reshape(r, A_WIDTH), ob.reshape(r, G_WIDTH), w_out.astype(_bf16),
      ln_g.reshape(1, -1), ln_b.reshape(1, -1))
    return out.reshape(b, s, d)
```

```python
import math

import jax
import jax.numpy as jnp
import numpy as np
from jax import lax
from jax.experimental import pallas as pl
from jax.experimental.pallas import tpu as pltpu

D_MODEL = 1024
HEADS = 8
NOPE = 64
ROPE = 32
HALF = ROPE // 2
POS_PER_ROW = 8
V_DIM = 64
A_WIDTH = HEADS * V_DIM
Q_LORA = 256
KV_LORA = 128
ROPE_THETA = 10000.0
CHUNK = 128
G_HEADS = 8
G_WIDTH = 512
G_HEAD_DIM = G_WIDTH // G_HEADS
DN_ALPHA = 2.0 ** 0.25
EPS = 1e-5
SM_SCALE = 1.0 / math.sqrt(NOPE + ROPE)
Q_SCALE = SM_SCALE * math.log2(math.e)

LANES = 128
HP = HEADS * LANES
ROPE_LO = NOPE

C_CQ = 0
C_CKV = C_CQ + Q_LORA
C_KR = C_CKV + KV_LORA
C_ZA = C_KR + LANES
C_U = C_ZA + A_WIDTH
C_V = C_U + G_WIDTH
C_ZB = C_V + G_WIDTH
C_END = C_ZB + G_WIDTH

TM_PROJ = 512
PROJ_CHUNK = 512
TM_OUT = 1024
OUT_CHUNK = 256
TQ = 1024
TK = 512
HEADS_PER_STEP = 2
NEG_BIG = -1e30

VMEM_LIMIT = 48 * 1024 * 1024
VMEM_LIMIT_ATTN = 52 * 1024 * 1024

_f32 = jnp.float32
_bf16 = jnp.bfloat16


def _dot(a, b):
    return jnp.dot(a, b, preferred_element_type=_f32)


def _dot_nt(a, b):
    return lax.dot_general(a, b, (((1,), (1,)), ((), ())), preferred_element_type=_f32)


def _gelu(x):
    return 0.5 * x * (1.0 + lax.erf(x * (1.0 / math.sqrt(2.0))))


def _silu(x):
    return x * (1.0 / (1.0 + jnp.exp(-x)))


def _proj_kernel(x_ref, pos_ref, aux_ref, win_ref, gq_ref, wuq_ref, gkv_ref, wuk_ref,
                 wuvt_ref, sg_ref, sb_ref, wsp_ref, bsp_ref,
                 q_out, k_out, vt_out, ga_out, ob_out, cos_sc, sup_sc, sdn_sc):
    tm = x_ref.shape[1]

    ang = pos_ref[0].astype(_f32) * aux_ref[0:1, :]
    cos8 = jnp.cos(ang)
    sin8 = jnp.sin(ang)
    lane = lax.broadcasted_iota(jnp.int32, cos8.shape, 1)
    in_lo = (lane >= ROPE_LO) & (lane < ROPE_LO + HALF)
    in_hi = (lane >= ROPE_LO + HALF) & (lane < ROPE_LO + ROPE)

    def place(t8, lane0, i):
        shift = (lane0 - HALF * i) % LANES
        return t8 if shift == 0 else pltpu.roll(t8, shift, 1)

    for i in range(POS_PER_ROW):
        rows = pl.ds(i, tm // POS_PER_ROW, stride=POS_PER_ROW)
        cos_sc[rows, :] = jnp.where(in_lo, place(cos8, ROPE_LO, i),
                                    jnp.where(in_hi, place(cos8, ROPE_LO + HALF, i), 1.0))
        sup_sc[rows, :] = jnp.where(in_hi, place(sin8, ROPE_LO + HALF, i), 0.0)
        sdn_sc[rows, :] = jnp.where(in_lo, -place(sin8, ROPE_LO, i), 0.0)

    def rms(c, g_ref):
        return c * lax.rsqrt(jnp.mean(c * c, axis=-1, keepdims=True) + EPS) * g_ref[...]

    row = lax.broadcasted_iota(jnp.int32, (CHUNK, 2 * CHUNK), 0)
    col = lax.broadcasted_iota(jnp.int32, (CHUNK, 2 * CHUNK), 1)
    tri = (col % CHUNK) <= row
    lo_half = lax.broadcasted_iota(jnp.int32, (CHUNK, LANES), 1) < G_HEAD_DIM
    ones_row = lax.broadcasted_iota(jnp.int32, (LANES, PROJ_CHUNK), 0) == V_DIM

    for rc in range(tm // PROJ_CHUNK):
        rs = slice(rc * PROJ_CHUNK, (rc + 1) * PROJ_CHUNK)
        x = x_ref[0, rs, :].astype(_bf16)

        def proj(lo, hi, x=x):
            return _dot(x, win_ref[:, lo:hi])

        def rope(t, rs=rs):
            return (t * cos_sc[rs, :] + pltpu.roll(t, HALF, 1) * sup_sc[rs, :]
                    + pltpu.roll(t, LANES - HALF, 1) * sdn_sc[rs, :])

        lat = proj(C_CQ, C_ZA)
        cqn = rms(lat[:, C_CQ:C_CKV], gq_ref).astype(_bf16)
        q = _dot(cqn, wuq_ref[...]) * Q_SCALE
        ckn = rms(lat[:, C_CKV:C_KR], gkv_ref).astype(_bf16)
        kn = _dot(ckn, wuk_ref[...])
        vt = _dot_nt(wuvt_ref[...], ckn)
        kr = rope(lat[:, C_KR:C_ZA])
        for h in range(HEADS):
            sl = slice(h * LANES, (h + 1) * LANES)
            q_out[0, h, rs, :] = rope(q[:, sl]).astype(_bf16)
            k_out[0, h, rs, :] = (kn[:, sl] + kr).astype(_bf16)
            vt_out[0, h, :, rs] = jnp.where(ones_row, 1.0, vt[sl, :]).astype(_bf16)

        ga_out[0, rs, :] = _silu(proj(C_ZA, C_U)).astype(_bf16)

        u = _gelu(proj(C_U, C_V))
        v = _gelu(proj(C_V, C_ZB))
        mu = jnp.mean(v, axis=-1, keepdims=True)
        vc = v - mu
        var = jnp.mean(vc * vc, axis=-1, keepdims=True)
        vn = (vc * lax.rsqrt(var + EPS) * sg_ref[...] + sb_ref[...])
        gb = _silu(proj(C_ZB, C_END))

        for pair in range(G_HEADS // 2):
            w_pair = jnp.where(tri, wsp_ref[pair], 0.0).astype(_bf16)
            ls = slice(pair * LANES, (pair + 1) * LANES)
            for c in range(PROJ_CHUNK // CHUNK):
                cs = slice(c * CHUNK, (c + 1) * CHUNK)
                os = slice(rc * PROJ_CHUNK + c * CHUNK, rc * PROJ_CHUNK + (c + 1) * CHUNK)
                vp = vn[cs, ls]
                stacked = jnp.concatenate(
                    [jnp.where(lo_half, vp, 0.0), jnp.where(lo_half, 0.0, vp)], axis=0
                ).astype(_bf16)
                sv = _dot(w_pair, stacked) + bsp_ref[:, ls]
                ob_out[0, os, ls] = (u[cs, ls] * sv * gb[cs, ls]).astype(_bf16)


def _attn_kernel(q_ref, k_ref, vt_ref, g_ref, o_ref, s_sc, cmax_sc, m_sc, acc_sc):
    seq = q_ref.shape[2]
    nq = seq // TQ
    low = TK

    def causal(s):
        r = lax.broadcasted_iota(jnp.int32, s.shape, 0)
        c = lax.broadcasted_iota(jnp.int32, s.shape, 1)
        return jnp.where(r <= c, s, NEG_BIG)

    def scores(hh, slot, q0, j, c0=0, masked=False):
        ks = pl.multiple_of(j * TK, TK)
        k = k_ref[0, hh, pl.ds(ks, TK), :]
        q = q_ref[0, hh, pl.ds(pl.multiple_of(q0 + c0, TK), TQ - c0), :]
        s = _dot_nt(k, q)
        if masked:
            s = causal(s)
        s_sc[hh, slot, :, c0:] = s
        cmax_sc[hh, slot, :, c0:] = jnp.max(s, axis=0, keepdims=True)

    def softmax_pv(hh, slot, j, c0=0, mask_here=False):
        ks = pl.multiple_of(j * TK, TK)
        s = s_sc[hh, slot, :, c0:]
        if mask_here:
            s = causal(s)
            m_cur = jnp.max(s, axis=0, keepdims=True)
        else:
            m_cur = cmax_sc[hh, slot, :, c0:]
        m_prev = m_sc[hh, :, c0:]
        m_next = jnp.maximum(m_prev, m_cur)
        p = jnp.exp2(s - m_next).astype(_bf16)
        alpha = jnp.exp2(m_prev - m_next)
        vt = vt_ref[0, hh, :, pl.ds(ks, TK)]
        acc_sc[hh, :, c0:] = alpha * acc_sc[hh, :, c0:] + _dot(vt, p)
        m_sc[hh, :, c0:] = m_next

    scores(0, 0, 0, 0)

    def q_tile(qi, carry):
        q0 = pl.multiple_of(qi * TQ, TQ)
        m_sc[...] = jnp.full(m_sc.shape, NEG_BIG, _f32)
        acc_sc[...] = jnp.zeros(acc_sc.shape, _f32)

        def two_blocks(j):
            scores(1, 0, q0, j)
            softmax_pv(0, 0, j)
            scores(0, 1, q0, j + 1)
            softmax_pv(1, 0, j)
            scores(1, 1, q0, j + 1)
            softmax_pv(0, 1, j + 1)
            scores(0, 0, q0, j + 2)
            softmax_pv(1, 1, j + 1)

        def four_blocks(i, c):
            two_blocks(4 * i)
            two_blocks(4 * i + 2)
            return c

        def last_two_blocks(i, c):
            two_blocks(4 * (qi // 2))
            return c

        lax.fori_loop(0, qi // 2, four_blocks, 0)
        lax.fori_loop(0, qi % 2, last_two_blocks, 0)
        j0 = (TQ // TK) * qi
        scores(1, 0, q0, j0, masked=True)
        softmax_pv(0, 0, j0, mask_here=True)
        scores(0, 1, q0, j0 + 1, low, masked=True)
        softmax_pv(1, 0, j0)
        scores(1, 1, q0, j0 + 1, low, masked=True)
        softmax_pv(0, 1, j0 + 1, low)
        scores(0, 0, jnp.minimum(qi + 1, nq - 1) * TQ, 0)
        softmax_pv(1, 1, j0 + 1, low)

        outs = []
        for hh in range(HEADS_PER_STEP):
            acc = acc_sc[hh]
            outs.append((acc * (1.0 / acc[V_DIM:V_DIM + 1, :])).T)
        lane = lax.broadcasted_iota(jnp.int32, (TQ, LANES), 1)
        both = jnp.where(lane < V_DIM, outs[0], pltpu.roll(outs[1], V_DIM, 1))
        rows = pl.ds(q0, TQ)
        o_ref[0, rows, :] = (both * g_ref[0, rows, :].astype(_f32)).astype(_bf16)
        return carry

    lax.fori_loop(0, nq, q_tile, 0)


def _out_kernel(x_ref, oa_ref, ob_ref, wo_ref, g_ref, b_ref, o_ref):
    for c in range(TM_OUT // OUT_CHUNK):
        rs = slice(c * OUT_CHUNK, (c + 1) * OUT_CHUNK)
        y = _dot(oa_ref[rs, :], wo_ref[:A_WIDTH, :]) + _dot(ob_ref[rs, :], wo_ref[A_WIDTH:, :])
        z = DN_ALPHA * x_ref[rs, :] + y
        mu = jnp.mean(z, axis=-1, keepdims=True)
        zc = z - mu
        var = jnp.mean(zc * zc, axis=-1, keepdims=True)
        o_ref[rs, :] = zc * lax.rsqrt(var + EPS) * g_ref[...] + b_ref[...]


def _prep_weights(w_in, w_uq, w_ukv, w_spatial, b_spatial):
    kr0 = Q_LORA + KV_LORA
    zeros = lambda n: jnp.zeros((D_MODEL, n), w_in.dtype)
    win_p = jnp.concatenate([w_in[:, :kr0], zeros(ROPE_LO), w_in[:, kr0:kr0 + ROPE],
                             zeros(LANES - ROPE_LO - ROPE), w_in[:, kr0 + ROPE:]],
                            axis=1).astype(_bf16)

    wq = w_uq.reshape(Q_LORA, HEADS, NOPE + ROPE)
    wq = jnp.pad(wq, ((0, 0), (0, 0), (0, LANES - NOPE - ROPE))).reshape(Q_LORA, HP)
    wkv = w_ukv.reshape(KV_LORA, HEADS, NOPE + V_DIM)
    wk = jnp.pad(wkv[:, :, :NOPE], ((0, 0), (0, 0), (0, LANES - NOPE))).reshape(KV_LORA, HP)
    wvt = jnp.pad(wkv[:, :, NOPE:], ((0, 0), (0, 0), (0, LANES - V_DIM))).reshape(KV_LORA, HP).T

    wsp = w_spatial.reshape(G_HEADS // 2, 2, CHUNK, CHUNK).transpose(0, 2, 1, 3)
    wsp = wsp.reshape(G_HEADS // 2, CHUNK, 2 * CHUNK)
    bsp = jnp.repeat(b_spatial.T, G_HEAD_DIM, axis=1)
    return win_p, wq.astype(_bf16), wk.astype(_bf16), wvt.astype(_bf16), wsp, bsp


def _aux_table():
    inv_freq = 1.0 / (ROPE_THETA ** (jnp.arange(HALF, dtype=_f32) / HALF))
    freq = jnp.tile(inv_freq, POS_PER_ROW)
    return jnp.concatenate([freq[None], jnp.zeros((7, LANES), _f32)], axis=0)


def _pack_positions(positions):
    b, s = positions.shape
    return jnp.repeat(positions.reshape(b, s // POS_PER_ROW, POS_PER_ROW), HALF, axis=-1)


def kernel(x, positions, w_in, q_norm_g, w_uq, kv_norm_g, w_ukv, sgu_norm_g, sgu_norm_b,
           w_spatial, b_spatial, w_out, ln_g, ln_b):
    b, s, d = x.shape
    assert d == D_MODEL and s % TQ == 0 and s % TM_PROJ == 0 and TQ == 2 * TK and HEADS_PER_STEP == 2
    win_p, wq, wk, wvt, wsp, bsp = _prep_weights(w_in, w_uq, w_ukv, w_spatial, b_spatial)
    aux = _aux_table()
    pos8 = _pack_positions(positions)

    const2 = lambda shape: pl.BlockSpec(shape, lambda bi, i: (0, 0))
    const3 = lambda shape: pl.BlockSpec(shape, lambda bi, i: (0, 0, 0))
    head_spec = pl.BlockSpec((1, HEADS, TM_PROJ, LANES), lambda bi, i: (bi, 0, i, 0))
    row_spec = lambda w: pl.BlockSpec((1, TM_PROJ, w), lambda bi, i: (bi, i, 0))
    hshape = jax.ShapeDtypeStruct((b, HEADS, s, LANES), _bf16)
    qp, kp, vtp, ga, ob = pl.pallas_call(
        _proj_kernel,
        grid=(b, s // TM_PROJ),
        in_specs=[row_spec(D_MODEL),
                  pl.BlockSpec((1, TM_PROJ // POS_PER_ROW, LANES), lambda bi, i: (bi, i, 0)),
                  const2((8, LANES)),
                  const2((D_MODEL, C_END)), const2((1, Q_LORA)), const2((Q_LORA, HP)),
                  const2((1, KV_LORA)), const2((KV_LORA, HP)), const2((HP, KV_LORA)),
                  const2((1, G_WIDTH)), const2((1, G_WIDTH)),
                  const3((G_HEADS // 2, CHUNK, 2 * CHUNK)), const2((CHUNK, G_WIDTH))],
        out_specs=[head_spec, head_spec,
                   pl.BlockSpec((1, HEADS, LANES, TM_PROJ), lambda bi, i: (bi, 0, 0, i)),
                   row_spec(A_WIDTH), row_spec(G_WIDTH)],
        out_shape=[hshape, hshape, jax.ShapeDtypeStruct((b, HEADS, LANES, s), _bf16),
                   jax.ShapeDtypeStruct((b, s, A_WIDTH), _bf16),
                   jax.ShapeDtypeStruct((b, s, G_WIDTH), _bf16)],
        scratch_shapes=[pltpu.VMEM((TM_PROJ, LANES), _f32)] * 3,
        compiler_params=pltpu.CompilerParams(
            dimension_semantics=("arbitrary", "arbitrary"), vmem_limit_bytes=VMEM_LIMIT),
        name="hyb_proj",
    )(x, pos8, aux, win_p, q_norm_g.reshape(1, -1), wq, kv_norm_g.reshape(1, -1), wk, wvt,
      sgu_norm_g.reshape(1, -1), sgu_norm_b.reshape(1, -1), wsp, bsp)

    hps = HEADS_PER_STEP
    per_pair = pl.BlockSpec((1, hps, s, LANES), lambda bi, hp: (bi, hp, 0, 0))
    gate_spec = pl.BlockSpec((1, s, LANES), lambda bi, hp: (bi, 0, hp))
    oa = pl.pallas_call(
        _attn_kernel,
        grid=(b, HEADS // hps),
        in_specs=[per_pair, per_pair,
                  pl.BlockSpec((1, hps, LANES, s), lambda bi, hp: (bi, hp, 0, 0)), gate_spec],
        out_specs=gate_spec,
        out_shape=jax.ShapeDtypeStruct((b, s, A_WIDTH), _bf16),
        scratch_shapes=[pltpu.VMEM((hps, 2, TK, TQ), _f32), pltpu.VMEM((hps, 2, 1, TQ), _f32),
                        pltpu.VMEM((hps, 1, TQ), _f32), pltpu.VMEM((hps, LANES, TQ), _f32)],
        compiler_params=pltpu.CompilerParams(
            dimension_semantics=("arbitrary", "arbitrary"), vmem_limit_bytes=VMEM_LIMIT_ATTN),
        name="hyb_attn",
    )(qp, kp, vtp, ga)

    r = b * s
    rows = lambda w: pl.BlockSpec((TM_OUT, w), lambda i: (i, 0))
    fixed = lambda shape: pl.BlockSpec(shape, lambda i: (0, 0))
    out = pl.pallas_call(
        _out_kernel,
        grid=(r // TM_OUT,),
        in_specs=[rows(D_MODEL), rows(A_WIDTH), rows(G_WIDTH), fixed((D_MODEL, D_MODEL)),
                  fixed((1, D_MODEL)), fixed((1, D_MODEL))],
        out_specs=rows(D_MODEL),
        out_shape=jax.ShapeDtypeStruct((r, D_MODEL), x.dtype),
        compiler_params=pltpu.CompilerParams(
            dimension_semantics=("arbitrary",), vmem_limit_bytes=VMEM_LIMIT),
        name="hyb_out",
    )(x.reshape(r, d), oa.reshape(r, A_WIDTH), ob.reshape(r, G_WIDTH), w_out.astype(_bf16),
      ln_g.reshape(1, -1), ln_b.reshape(1, -1))
    return out.reshape(b, s, d)
```

```python
import math

import jax
import jax.numpy as jnp
import numpy as np
from jax import lax
from jax.experimental import pallas as pl
from jax.experimental.pallas import tpu as pltpu

D_MODEL = 1024
HEADS = 8
NOPE = 64
ROPE = 32
HALF = ROPE // 2
POS_PER_ROW = 8
V_DIM = 64
ONES_ROW = (V_DIM, 0)
A_WIDTH = HEADS * V_DIM
Q_LORA = 256
KV_LORA = 128
ROPE_THETA = 10000.0
CHUNK = 128
G_HEADS = 8
G_WIDTH = 512
G_HEAD_DIM = G_WIDTH // G_HEADS
DN_ALPHA = 2.0 ** 0.25
EPS = 1e-5
SM_SCALE = 1.0 / math.sqrt(NOPE + ROPE)
Q_SCALE = SM_SCALE * math.log2(math.e)

LANES = 128
HP = HEADS * LANES
ROPE_LO = NOPE

C_CQ = 0
C_CKV = C_CQ + Q_LORA
C_KR = C_CKV + KV_LORA
C_ZA = C_KR + LANES
C_U = C_ZA + A_WIDTH
C_V = C_U + G_WIDTH
C_ZB = C_V + G_WIDTH
C_END = C_ZB + G_WIDTH

TM_PROJ = 512
PROJ_CHUNK = 512
TM_OUT = 1024
OUT_CHUNK = 256
TQ = 1024
TK = 512
HEADS_PER_STEP = 2
NEG_BIG = -1e30

VMEM_LIMIT = 48 * 1024 * 1024
VMEM_LIMIT_ATTN = 52 * 1024 * 1024

_f32 = jnp.float32
_bf16 = jnp.bfloat16


def _dot(a, b):
    return jnp.dot(a, b, preferred_element_type=_f32)


def _dot_nt(a, b):
    return lax.dot_general(a, b, (((1,), (1,)), ((), ())), preferred_element_type=_f32)


def _gelu(x):
    return 0.5 * x * (1.0 + lax.erf(x * (1.0 / math.sqrt(2.0))))


def _silu(x):
    return x * (1.0 / (1.0 + jnp.exp(-x)))


def _proj_kernel(x_ref, pos_ref, aux_ref, win_ref, gq_ref, wuq_ref, gkv_ref, wuk_ref,
                 wuvt_ref, sg_ref, sb_ref, wsp_ref, bsp_ref,
                 q_out, k_out, vt_out, ga_out, ob_out, cos_sc, sup_sc, sdn_sc):
    tm = x_ref.shape[1]

    ang = pos_ref[0].astype(_f32) * aux_ref[0:1, :]
    cos8 = jnp.cos(ang)
    sin8 = jnp.sin(ang)
    lane = lax.broadcasted_iota(jnp.int32, cos8.shape, 1)
    in_lo = (lane >= ROPE_LO) & (lane < ROPE_LO + HALF)
    in_hi = (lane >= ROPE_LO + HALF) & (lane < ROPE_LO + ROPE)

    def place(t8, lane0, i):
        shift = (lane0 - HALF * i) % LANES
        return t8 if shift == 0 else pltpu.roll(t8, shift, 1)

    for i in range(POS_PER_ROW):
        rows = pl.ds(i, tm // POS_PER_ROW, stride=POS_PER_ROW)
        cos_sc[rows, :] = jnp.where(in_lo, place(cos8, ROPE_LO, i),
                                    jnp.where(in_hi, place(cos8, ROPE_LO + HALF, i), 1.0))
        sup_sc[rows, :] = jnp.where(in_hi, place(sin8, ROPE_LO + HALF, i), 0.0)
        sdn_sc[rows, :] = jnp.where(in_lo, -place(sin8, ROPE_LO, i), 0.0)

    def rms(c, g_ref):
        return c * lax.rsqrt(jnp.mean(c * c, axis=-1, keepdims=True) + EPS) * g_ref[...]

    row = lax.broadcasted_iota(jnp.int32, (CHUNK, 2 * CHUNK), 0)
    col = lax.broadcasted_iota(jnp.int32, (CHUNK, 2 * CHUNK), 1)
    tri = (col % CHUNK) <= row
    lo_half = lax.broadcasted_iota(jnp.int32, (CHUNK, LANES), 1) < G_HEAD_DIM
    vt_row = lax.broadcasted_iota(jnp.int32, (LANES, PROJ_CHUNK), 0)
    ones_row = [vt_row == ONES_ROW[0], vt_row == ONES_ROW[1]]

    for rc in range(tm // PROJ_CHUNK):
        rs = slice(rc * PROJ_CHUNK, (rc + 1) * PROJ_CHUNK)
        x = x_ref[0, rs, :].astype(_bf16)

        def proj(lo, hi, x=x):
            return _dot(x, win_ref[:, lo:hi])

        def rope(t, rs=rs):
            return (t * cos_sc[rs, :] + pltpu.roll(t, HALF, 1) * sup_sc[rs, :]
                    + pltpu.roll(t, LANES - HALF, 1) * sdn_sc[rs, :])

        lat = proj(C_CQ, C_ZA)
        cqn = rms(lat[:, C_CQ:C_CKV], gq_ref).astype(_bf16)
        q = _dot(cqn, wuq_ref[...]) * Q_SCALE
        ckn = rms(lat[:, C_CKV:C_KR], gkv_ref).astype(_bf16)
        kn = _dot(ckn, wuk_ref[...])
        vt = _dot_nt(wuvt_ref[...], ckn)
        kr = rope(lat[:, C_KR:C_ZA])
        for h in range(HEADS):
            sl = slice(h * LANES, (h + 1) * LANES)
            q_out[0, h, rs, :] = rope(q[:, sl]).astype(_bf16)
            k_out[0, h, rs, :] = (kn[:, sl] + kr).astype(_bf16)
            vt_out[0, h, :, rs] = jnp.where(ones_row[h % 2], 1.0, vt[sl, :]).astype(_bf16)

        ga_out[0, rs, :] = _silu(proj(C_ZA, C_U)).astype(_bf16)

        u = _gelu(proj(C_U, C_V))
        v = _gelu(proj(C_V, C_ZB))
        mu = jnp.mean(v, axis=-1, keepdims=True)
        vc = v - mu
        var = jnp.mean(vc * vc, axis=-1, keepdims=True)
        vn = (vc * lax.rsqrt(var + EPS) * sg_ref[...] + sb_ref[...])
        gb = _silu(proj(C_ZB, C_END))

        for pair in range(G_HEADS // 2):
            w_pair = jnp.where(tri, wsp_ref[pair], 0.0).astype(_bf16)
            ls = slice(pair * LANES, (pair + 1) * LANES)
            for c in range(PROJ_CHUNK // CHUNK):
                cs = slice(c * CHUNK, (c + 1) * CHUNK)
                os = slice(rc * PROJ_CHUNK + c * CHUNK, rc * PROJ_CHUNK + (c + 1) * CHUNK)
                vp = vn[cs, ls]
                stacked = jnp.concatenate(
                    [jnp.where(lo_half, vp, 0.0), jnp.where(lo_half, 0.0, vp)], axis=0
                ).astype(_bf16)
                sv = _dot(w_pair, stacked) + bsp_ref[:, ls]
                ob_out[0, os, ls] = (u[cs, ls] * sv * gb[cs, ls]).astype(_bf16)


def _attn_kernel(q_ref, k_ref, vt_ref, g_ref, o_ref, s_sc, cmax_sc, m_sc, acc_sc):
    seq = q_ref.shape[2]
    nq = seq // TQ
    low = TK

    def causal(s):
        r = lax.broadcasted_iota(jnp.int32, s.shape, 0)
        c = lax.broadcasted_iota(jnp.int32, s.shape, 1)
        return jnp.where(r <= c, s, NEG_BIG)

    def scores(hh, slot, q0, j, c0=0, masked=False):
        ks = pl.multiple_of(j * TK, TK)
        k = k_ref[0, hh, pl.ds(ks, TK), :]
        q = q_ref[0, hh, pl.ds(pl.multiple_of(q0 + c0, TK), TQ - c0), :]
        s = _dot_nt(k, q)
        if masked:
            s = causal(s)
        s_sc[hh, slot, :, c0:] = s
        cmax_sc[hh, slot, :, c0:] = jnp.max(s, axis=0, keepdims=True)

    def softmax_pv(hh, slot, j, c0=0, mask_here=False):
        ks = pl.multiple_of(j * TK, TK)
        s = s_sc[hh, slot, :, c0:]
        if mask_here:
            s = causal(s)
            m_cur = jnp.max(s, axis=0, keepdims=True)
        else:
            m_cur = cmax_sc[hh, slot, :, c0:]
        m_prev = m_sc[hh, :, c0:]
        m_next = jnp.maximum(m_prev, m_cur)
        p = jnp.exp2(s - m_next).astype(_bf16)
        alpha = jnp.exp2(m_prev - m_next)
        vt = vt_ref[0, hh, :, pl.ds(ks, TK)]
        acc_sc[hh, :, c0:] = alpha * acc_sc[hh, :, c0:] + _dot(vt, p)
        m_sc[hh, :, c0:] = m_next

    scores(0, 0, 0, 0)

    def q_tile(qi, carry):
        q0 = pl.multiple_of(qi * TQ, TQ)
        m_sc[...] = jnp.full(m_sc.shape, NEG_BIG, _f32)
        acc_sc[...] = jnp.zeros(acc_sc.shape, _f32)

        def two_blocks(j):
            scores(1, 0, q0, j)
            softmax_pv(0, 0, j)
            scores(0, 1, q0, j + 1)
            softmax_pv(1, 0, j)
            scores(1, 1, q0, j + 1)
            softmax_pv(0, 1, j + 1)
            scores(0, 0, q0, j + 2)
            softmax_pv(1, 1, j + 1)

        def four_blocks(i, c):
            two_blocks(4 * i)
            two_blocks(4 * i + 2)
            return c

        def last_two_blocks(i, c):
            two_blocks(4 * (qi // 2))
            return c

        lax.fori_loop(0, qi // 2, four_blocks, 0)
        lax.fori_loop(0, qi % 2, last_two_blocks, 0)
        j0 = (TQ // TK) * qi
        scores(1, 0, q0, j0, masked=True)
        softmax_pv(0, 0, j0, mask_here=True)
        scores(0, 1, q0, j0 + 1, low, masked=True)
        softmax_pv(1, 0, j0)
        scores(1, 1, q0, j0 + 1, low, masked=True)
        softmax_pv(0, 1, j0 + 1, low)
        scores(0, 0, jnp.minimum(qi + 1, nq - 1) * TQ, 0)
        softmax_pv(1, 1, j0 + 1, low)

        acc0, acc1 = acc_sc[0], acc_sc[1]
        o0 = acc0 * (1.0 / acc0[ONES_ROW[0]:ONES_ROW[0] + 1, :])
        o1 = acc1 * (1.0 / acc1[ONES_ROW[1]:ONES_ROW[1] + 1, :])
        row = lax.broadcasted_iota(jnp.int32, (LANES, TQ), 0)
        both = jnp.where(row < V_DIM, o0, o1).T
        rows = pl.ds(q0, TQ)
        o_ref[0, rows, :] = (both * g_ref[0, rows, :].astype(_f32)).astype(_bf16)
        return carry

    lax.fori_loop(0, nq, q_tile, 0)


def _out_kernel(x_ref, oa_ref, ob_ref, wo_ref, g_ref, b_ref, o_ref):
    for c in range(TM_OUT // OUT_CHUNK):
        rs = slice(c * OUT_CHUNK, (c + 1) * OUT_CHUNK)
        y = _dot(oa_ref[rs, :], wo_ref[:A_WIDTH, :]) + _dot(ob_ref[rs, :], wo_ref[A_WIDTH:, :])
        z = DN_ALPHA * x_ref[rs, :] + y
        mu = jnp.mean(z, axis=-1, keepdims=True)
        zc = z - mu
        var = jnp.mean(zc * zc, axis=-1, keepdims=True)
        o_ref[rs, :] = zc * lax.rsqrt(var + EPS) * g_ref[...] + b_ref[...]


def _prep_weights(w_in, w_uq, w_ukv, w_spatial, b_spatial):
    kr0 = Q_LORA + KV_LORA
    zeros = lambda n: jnp.zeros((D_MODEL, n), w_in.dtype)
    win_p = jnp.concatenate([w_in[:, :kr0], zeros(ROPE_LO), w_in[:, kr0:kr0 + ROPE],
                             zeros(LANES - ROPE_LO - ROPE), w_in[:, kr0 + ROPE:]],
                            axis=1).astype(_bf16)

    wq = w_uq.reshape(Q_LORA, HEADS, NOPE + ROPE)
    wq = jnp.pad(wq, ((0, 0), (0, 0), (0, LANES - NOPE - ROPE))).reshape(Q_LORA, HP)
    wkv = w_ukv.reshape(KV_LORA, HEADS, NOPE + V_DIM)
    wk = jnp.pad(wkv[:, :, :NOPE], ((0, 0), (0, 0), (0, LANES - NOPE))).reshape(KV_LORA, HP)
    wv = wkv[:, :, NOPE:]
    wv_even = jnp.pad(wv[:, 0::2], ((0, 0), (0, 0), (0, LANES - V_DIM)))
    wv_odd = jnp.pad(wv[:, 1::2], ((0, 0), (0, 0), (LANES - V_DIM, 0)))
    wvt = jnp.stack([wv_even, wv_odd], axis=2).reshape(KV_LORA, HP).T

    wsp = w_spatial.reshape(G_HEADS // 2, 2, CHUNK, CHUNK).transpose(0, 2, 1, 3)
    wsp = wsp.reshape(G_HEADS // 2, CHUNK, 2 * CHUNK)
    bsp = jnp.repeat(b_spatial.T, G_HEAD_DIM, axis=1)
    return win_p, wq.astype(_bf16), wk.astype(_bf16), wvt.astype(_bf16), wsp, bsp


def _aux_table():
    inv_freq = 1.0 / (ROPE_THETA ** (jnp.arange(HALF, dtype=_f32) / HALF))
    freq = jnp.tile(inv_freq, POS_PER_ROW)
    return jnp.concatenate([freq[None], jnp.zeros((7, LANES), _f32)], axis=0)


def _pack_positions(positions):
    b, s = positions.shape
    return jnp.repeat(positions.reshape(b, s // POS_PER_ROW, POS_PER_ROW), HALF, axis=-1)


def kernel(x, positions, w_in, q_norm_g, w_uq, kv_norm_g, w_ukv, sgu_norm_g, sgu_norm_b,
           w_spatial, b_spatial, w_out, ln_g, ln_b):
    b, s, d = x.shape
    assert d == D_MODEL and s % TQ == 0 and s % TM_PROJ == 0 and TQ == 2 * TK and HEADS_PER_STEP == 2
    win_p, wq, wk, wvt, wsp, bsp = _prep_weights(w_in, w_uq, w_ukv, w_spatial, b_spatial)
    aux = _aux_table()
    pos8 = _pack_positions(positions)

    const2 = lambda shape: pl.BlockSpec(shape, lambda bi, i: (0, 0))
    const3 = lambda shape: pl.BlockSpec(shape, lambda bi, i: (0, 0, 0))
    head_spec = pl.BlockSpec((1, HEADS, TM_PROJ, LANES), lambda bi, i: (bi, 0, i, 0))
    row_spec = lambda w: pl.BlockSpec((1, TM_PROJ, w), lambda bi, i: (bi, i, 0))
    hshape = jax.ShapeDtypeStruct((b, HEADS, s, LANES), _bf16)
    qp, kp, vtp, ga, ob = pl.pallas_call(
        _proj_kernel,
        grid=(b, s // TM_PROJ),
        in_specs=[row_spec(D_MODEL),
                  pl.BlockSpec((1, TM_PROJ // POS_PER_ROW, LANES), lambda bi, i: (bi, i, 0)),
                  const2((8, LANES)),
                  const2((D_MODEL, C_END)), const2((1, Q_LORA)), const2((Q_LORA, HP)),
                  const2((1, KV_LORA)), const2((KV_LORA, HP)), const2((HP, KV_LORA)),
                  const2((1, G_WIDTH)), const2((1, G_WIDTH)),
                  const3((G_HEADS // 2, CHUNK, 2 * CHUNK)), const2((CHUNK, G_WIDTH))],
        out_specs=[head_spec, head_spec,
                   pl.BlockSpec((1, HEADS, LANES, TM_PROJ), lambda bi, i: (bi, 0, 0, i)),
                   row_spec(A_WIDTH), row_spec(G_WIDTH)],
        out_shape=[hshape, hshape, jax.ShapeDtypeStruct((b, HEADS, LANES, s), _bf16),
                   jax.ShapeDtypeStruct((b, s, A_WIDTH), _bf16),
                   jax.ShapeDtypeStruct((b, s, G_WIDTH), _bf16)],
        scratch_shapes=[pltpu.VMEM((TM_PROJ, LANES), _f32)] * 3,
        compiler_params=pltpu.CompilerParams(
            dimension_semantics=("arbitrary", "arbitrary"), vmem_limit_bytes=VMEM_LIMIT),
        name="hyb_proj",
    )(x, pos8, aux, win_p, q_norm_g.reshape(1, -1), wq, kv_norm_g.reshape(1, -1), wk, wvt,
      sgu_norm_g.reshape(1, -1), sgu_norm_b.reshape(1, -1), wsp, bsp)

    hps = HEADS_PER_STEP
    per_pair = pl.BlockSpec((1, hps, s, LANES), lambda bi, hp: (bi, hp, 0, 0))
    gate_spec = pl.BlockSpec((1, s, LANES), lambda bi, hp: (bi, 0, hp))
    oa = pl.pallas_call(
        _attn_kernel,
        grid=(b, HEADS // hps),
        in_specs=[per_pair, per_pair,
                  pl.BlockSpec((1, hps, LANES, s), lambda bi, hp: (bi, hp, 0, 0)), gate_spec],
        out_specs=gate_spec,
        out_shape=jax.ShapeDtypeStruct((b, s, A_WIDTH), _bf16),
        scratch_shapes=[pltpu.VMEM((hps, 2, TK, TQ), _f32), pltpu.VMEM((hps, 2, 1, TQ), _f32),
                        pltpu.VMEM((hps, 1, TQ), _f32), pltpu.VMEM((hps, LANES, TQ), _f32)],
        compiler_params=pltpu.CompilerParams(
            dimension_semantics=("arbitrary", "arbitrary"), vmem_limit_bytes=VMEM_LIMIT_ATTN),
        name="hyb_attn",
    )(qp, kp, vtp, ga)

    r = b * s
    rows = lambda w: pl.BlockSpec((TM_OUT, w), lambda i: (i, 0))
    fixed = lambda shape: pl.BlockSpec(shape, lambda i: (0, 0))
    out = pl.pallas_call(
        _out_kernel,
        grid=(r // TM_OUT,),
        in_specs=[rows(D_MODEL), rows(A_WIDTH), rows(G_WIDTH), fixed((D_MODEL, D_MODEL)),
                  fixed((1, D_MODEL)), fixed((1, D_MODEL))],
        out_specs=rows(D_MODEL),
        out_shape=jax.ShapeDtypeStruct((r, D_MODEL), x.dtype),
        compiler_params=pltpu.CompilerParams(
            dimension_semantics=("arbitrary",), vmem_limit_bytes=VMEM_LIMIT),
        name="hyb_out",
    )(x.reshape(r, d), oa.reshape(r, A_WIDTH), ob.reshape(r, G_WIDTH), w_out.astype(_bf16),
      ln_g.reshape(1, -1), ln_b.reshape(1, -1))
    return out.reshape(b, s, d)
```

```python
import math

import jax
import jax.numpy as jnp
import numpy as np
from jax import lax
from jax.experimental import pallas as pl
from jax.experimental.pallas import tpu as pltpu

D_MODEL = 1024
HEADS = 8
NOPE = 64
ROPE = 32
HALF = ROPE // 2
POS_PER_ROW = 8
V_DIM = 64
ONES_ROW = (V_DIM, 0)
A_WIDTH = HEADS * V_DIM
Q_LORA = 256
KV_LORA = 128
ROPE_THETA = 10000.0
CHUNK = 128
G_HEADS = 8
G_WIDTH = 512
G_HEAD_DIM = G_WIDTH // G_HEADS
DN_ALPHA = 2.0 ** 0.25
EPS = 1e-5
SM_SCALE = 1.0 / math.sqrt(NOPE + ROPE)
Q_SCALE = SM_SCALE * math.log2(math.e)

LANES = 128
HP = HEADS * LANES
ROPE_LO = NOPE

C_CQ = 0
C_CKV = C_CQ + Q_LORA
C_KR = C_CKV + KV_LORA
C_ZA = C_KR + LANES
C_U = C_ZA + A_WIDTH
C_V = C_U + G_WIDTH
C_ZB = C_V + G_WIDTH
C_END = C_ZB + G_WIDTH

TM_PROJ = 512
PROJ_CHUNK = 512
TM_OUT = 1024
OUT_CHUNK = 256
TQ = 1024
TK = 512
HEADS_PER_STEP = 2
NEG_BIG = -1e30

VMEM_LIMIT = 48 * 1024 * 1024
VMEM_LIMIT_ATTN = 52 * 1024 * 1024

_f32 = jnp.float32
_bf16 = jnp.bfloat16


def _dot(a, b):
    return jnp.dot(a, b, preferred_element_type=_f32)


def _dot_nt(a, b):
    return lax.dot_general(a, b, (((1,), (1,)), ((), ())), preferred_element_type=_f32)


def _gelu(x):
    return 0.5 * x * (1.0 + lax.erf(x * (1.0 / math.sqrt(2.0))))


def _silu(x):
    return x * (1.0 / (1.0 + jnp.exp(-x)))


def _proj_kernel(x_ref, pos_ref, aux_ref, win_ref, gq_ref, wuq_ref, gkv_ref, wuk_ref,
                 wuvt_ref, sg_ref, sb_ref, wsp_ref, bsp_ref,
                 q_out, k_out, vt_out, ga_out, ob_out, cos_sc, sup_sc, sdn_sc):
    tm = x_ref.shape[1]

    ang = pos_ref[0].astype(_f32) * aux_ref[0:1, :]
    cos8 = jnp.cos(ang)
    sin8 = jnp.sin(ang)
    lane = lax.broadcasted_iota(jnp.int32, cos8.shape, 1)
    in_lo = (lane >= ROPE_LO) & (lane < ROPE_LO + HALF)
    in_hi = (lane >= ROPE_LO + HALF) & (lane < ROPE_LO + ROPE)

    def place(t8, lane0, i):
        shift = (lane0 - HALF * i) % LANES
        return t8 if shift == 0 else pltpu.roll(t8, shift, 1)

    for i in range(POS_PER_ROW):
        rows = pl.ds(i, tm // POS_PER_ROW, stride=POS_PER_ROW)
        cos_sc[rows, :] = jnp.where(in_lo, place(cos8, ROPE_LO, i),
                                    jnp.where(in_hi, place(cos8, ROPE_LO + HALF, i), 1.0))
        sup_sc[rows, :] = jnp.where(in_hi, place(sin8, ROPE_LO + HALF, i), 0.0)
        sdn_sc[rows, :] = jnp.where(in_lo, -place(sin8, ROPE_LO, i), 0.0)

    def rms(c, g_ref):
        return c * lax.rsqrt(jnp.mean(c * c, axis=-1, keepdims=True) + EPS) * g_ref[...]

    row = lax.broadcasted_iota(jnp.int32, (CHUNK, 2 * CHUNK), 0)
    col = lax.broadcasted_iota(jnp.int32, (CHUNK, 2 * CHUNK), 1)
    tri = (col % CHUNK) <= row
    lo_half = lax.broadcasted_iota(jnp.int32, (CHUNK, LANES), 1) < G_HEAD_DIM
    vt_row = lax.broadcasted_iota(jnp.int32, (LANES, PROJ_CHUNK), 0)
    ones_row = [vt_row == ONES_ROW[0], vt_row == ONES_ROW[1]]

    for rc in range(tm // PROJ_CHUNK):
        rs = slice(rc * PROJ_CHUNK, (rc + 1) * PROJ_CHUNK)
        x = x_ref[0, rs, :].astype(_bf16)

        def proj(lo, hi, x=x):
            return _dot(x, win_ref[:, lo:hi])

        def rope(t, rs=rs):
            return (t * cos_sc[rs, :] + pltpu.roll(t, HALF, 1) * sup_sc[rs, :]
                    + pltpu.roll(t, LANES - HALF, 1) * sdn_sc[rs, :])

        lat = proj(C_CQ, C_ZA)
        v_pre = proj(C_V, C_ZB)
        u_pre = proj(C_U, C_V)
        cqn = rms(lat[:, C_CQ:C_CKV], gq_ref).astype(_bf16)
        q = _dot(cqn, wuq_ref[...]) * Q_SCALE
        ckn = rms(lat[:, C_CKV:C_KR], gkv_ref).astype(_bf16)
        kn = _dot(ckn, wuk_ref[...])
        vt = _dot_nt(wuvt_ref[...], ckn)
        kr = rope(lat[:, C_KR:C_ZA])
        for h in range(HEADS):
            sl = slice(h * LANES, (h + 1) * LANES)
            q_out[0, h, rs, :] = rope(q[:, sl]).astype(_bf16)
            k_out[0, h, rs, :] = (kn[:, sl] + kr).astype(_bf16)
            vt_out[0, h, :, rs] = jnp.where(ones_row[h % 2], 1.0, vt[sl, :]).astype(_bf16)

        zb_pre = proj(C_ZB, C_END)
        ga_out[0, rs, :] = _silu(proj(C_ZA, C_U)).astype(_bf16)

        u = _gelu(u_pre)
        v = _gelu(v_pre)
        mu = jnp.mean(v, axis=-1, keepdims=True)
        vc = v - mu
        var = jnp.mean(vc * vc, axis=-1, keepdims=True)
        vn = (vc * lax.rsqrt(var + EPS) * sg_ref[...] + sb_ref[...])
        gb = _silu(zb_pre)

        for pair in range(G_HEADS // 2):
            w_pair = jnp.where(tri, wsp_ref[pair], 0.0).astype(_bf16)
            ls = slice(pair * LANES, (pair + 1) * LANES)
            n_chunks = PROJ_CHUNK // CHUNK
            stacked = []
            for c in range(n_chunks):
                vp = vn[c * CHUNK:(c + 1) * CHUNK, ls]
                stacked.append(jnp.concatenate(
                    [jnp.where(lo_half, vp, 0.0), jnp.where(lo_half, 0.0, vp)], axis=0
                ).astype(_bf16))
            sv_all = _dot(w_pair, jnp.concatenate(stacked, axis=1))
            for c in range(n_chunks):
                cs = slice(c * CHUNK, (c + 1) * CHUNK)
                os = slice(rc * PROJ_CHUNK + c * CHUNK, rc * PROJ_CHUNK + (c + 1) * CHUNK)
                sv = sv_all[:, c * LANES:(c + 1) * LANES] + bsp_ref[:, ls]
                ob_out[0, os, ls] = (u[cs, ls] * sv * gb[cs, ls]).astype(_bf16)


def _attn_kernel(q_ref, k_ref, vt_ref, g_ref, o_ref, s_sc, cmax_sc, m_sc, acc_sc):
    seq = q_ref.shape[2]
    nq = seq // TQ
    low = TK

    def causal(s):
        r = lax.broadcasted_iota(jnp.int32, s.shape, 0)
        c = lax.broadcasted_iota(jnp.int32, s.shape, 1)
        return jnp.where(r <= c, s, NEG_BIG)

    def scores(hh, slot, q0, j, c0=0, masked=False):
        ks = pl.multiple_of(j * TK, TK)
        k = k_ref[0, hh, pl.ds(ks, TK), :]
        q = q_ref[0, hh, pl.ds(pl.multiple_of(q0 + c0, TK), TQ - c0), :]
        s = _dot_nt(k, q)
        if masked:
            s = causal(s)
        s_sc[hh, slot, :, c0:] = s
        cmax_sc[hh, slot, :, c0:] = jnp.max(s, axis=0, keepdims=True)

    def softmax_pv(hh, slot, j, c0=0, mask_here=False):
        ks = pl.multiple_of(j * TK, TK)
        s = s_sc[hh, slot, :, c0:]
        if mask_here:
            s = causal(s)
            m_cur = jnp.max(s, axis=0, keepdims=True)
        else:
            m_cur = cmax_sc[hh, slot, :, c0:]
        m_prev = m_sc[hh, :, c0:]
        m_next = jnp.maximum(m_prev, m_cur)
        p = jnp.exp2(s - m_next).astype(_bf16)
        alpha = jnp.exp2(m_prev - m_next)
        vt = vt_ref[0, hh, :, pl.ds(ks, TK)]
        acc_sc[hh, :, c0:] = alpha * acc_sc[hh, :, c0:] + _dot(vt, p)
        m_sc[hh, :, c0:] = m_next

    scores(0, 0, 0, 0)

    def q_tile(qi, carry):
        q0 = pl.multiple_of(qi * TQ, TQ)
        m_sc[...] = jnp.full(m_sc.shape, NEG_BIG, _f32)
        acc_sc[...] = jnp.zeros(acc_sc.shape, _f32)

        def two_blocks(j):
            scores(1, 0, q0, j)
            softmax_pv(0, 0, j)
            scores(0, 1, q0, j + 1)
            softmax_pv(1, 0, j)
            scores(1, 1, q0, j + 1)
            softmax_pv(0, 1, j + 1)
            scores(0, 0, q0, j + 2)
            softmax_pv(1, 1, j + 1)

        def four_blocks(i, c):
            two_blocks(4 * i)
            two_blocks(4 * i + 2)
            return c

        def last_two_blocks(i, c):
            two_blocks(4 * (qi // 2))
            return c

        lax.fori_loop(0, qi // 2, four_blocks, 0)
        lax.fori_loop(0, qi % 2, last_two_blocks, 0)
        j0 = (TQ // TK) * qi
        scores(1, 0, q0, j0, masked=True)
        softmax_pv(0, 0, j0, mask_here=True)
        scores(0, 1, q0, j0 + 1, low, masked=True)
        softmax_pv(1, 0, j0)
        scores(1, 1, q0, j0 + 1, low, masked=True)
        softmax_pv(0, 1, j0 + 1, low)
        scores(0, 0, jnp.minimum(qi + 1, nq - 1) * TQ, 0)
        softmax_pv(1, 1, j0 + 1, low)

        acc0, acc1 = acc_sc[0], acc_sc[1]
        o0 = acc0 * (1.0 / acc0[ONES_ROW[0]:ONES_ROW[0] + 1, :])
        o1 = acc1 * (1.0 / acc1[ONES_ROW[1]:ONES_ROW[1] + 1, :])
        row = lax.broadcasted_iota(jnp.int32, (LANES, TQ), 0)
        both = jnp.where(row < V_DIM, o0, o1).T
        rows = pl.ds(q0, TQ)
        o_ref[0, rows, :] = (both * g_ref[0, rows, :].astype(_f32)).astype(_bf16)
        return carry

    lax.fori_loop(0, nq, q_tile, 0)


def _out_kernel(x_ref, oa_ref, ob_ref, wo_ref, g_ref, b_ref, o_ref):
    for c in range(TM_OUT // OUT_CHUNK):
        rs = slice(c * OUT_CHUNK, (c + 1) * OUT_CHUNK)
        y = _dot(oa_ref[rs, :], wo_ref[:A_WIDTH, :]) + _dot(ob_ref[rs, :], wo_ref[A_WIDTH:, :])
        z = DN_ALPHA * x_ref[rs, :] + y
        mu = jnp.mean(z, axis=-1, keepdims=True)
        zc = z - mu
        var = jnp.mean(zc * zc, axis=-1, keepdims=True)
        o_ref[rs, :] = zc * lax.rsqrt(var + EPS) * g_ref[...] + b_ref[...]


def _prep_weights(w_in, w_uq, w_ukv, w_spatial, b_spatial):
    kr0 = Q_LORA + KV_LORA
    zeros = lambda n: jnp.zeros((D_MODEL, n), w_in.dtype)
    win_p = jnp.concatenate([w_in[:, :kr0], zeros(ROPE_LO), w_in[:, kr0:kr0 + ROPE],
                             zeros(LANES - ROPE_LO - ROPE), w_in[:, kr0 + ROPE:]],
                            axis=1).astype(_bf16)

    wq = w_uq.reshape(Q_LORA, HEADS, NOPE + ROPE)
    wq = jnp.pad(wq, ((0, 0), (0, 0), (0, LANES - NOPE - ROPE))).reshape(Q_LORA, HP)
    wkv = w_ukv.reshape(KV_LORA, HEADS, NOPE + V_DIM)
    wk = jnp.pad(wkv[:, :, :NOPE], ((0, 0), (0, 0), (0, LANES - NOPE))).reshape(KV_LORA, HP)
    wv = wkv[:, :, NOPE:]
    wv_even = jnp.pad(wv[:, 0::2], ((0, 0), (0, 0), (0, LANES - V_DIM)))
    wv_odd = jnp.pad(wv[:, 1::2], ((0, 0), (0, 0), (LANES - V_DIM, 0)))
    wvt = jnp.stack([wv_even, wv_odd], axis=2).reshape(KV_LORA, HP).T

    wsp = w_spatial.reshape(G_HEADS // 2, 2, CHUNK, CHUNK).transpose(0, 2, 1, 3)
    wsp = wsp.reshape(G_HEADS // 2, CHUNK, 2 * CHUNK)
    bsp = jnp.repeat(b_spatial.T, G_HEAD_DIM, axis=1)
    return win_p, wq.astype(_bf16), wk.astype(_bf16), wvt.astype(_bf16), wsp, bsp


def _aux_table():
    inv_freq = 1.0 / (ROPE_THETA ** (jnp.arange(HALF, dtype=_f32) / HALF))
    freq = jnp.tile(inv_freq, POS_PER_ROW)
    return jnp.concatenate([freq[None], jnp.zeros((7, LANES), _f32)], axis=0)


def _pack_positions(positions):
    b, s = positions.shape
    return jnp.repeat(positions.reshape(b, s // POS_PER_ROW, POS_PER_ROW), HALF, axis=-1)


def kernel(x, positions, w_in, q_norm_g, w_uq, kv_norm_g, w_ukv, sgu_norm_g, sgu_norm_b,
           w_spatial, b_spatial, w_out, ln_g, ln_b):
    b, s, d = x.shape
    assert d == D_MODEL and s % TQ == 0 and s % TM_PROJ == 0 and TQ == 2 * TK and HEADS_PER_STEP == 2
    win_p, wq, wk, wvt, wsp, bsp = _prep_weights(w_in, w_uq, w_ukv, w_spatial, b_spatial)
    aux = _aux_table()
    pos8 = _pack_positions(positions)

    const2 = lambda shape: pl.BlockSpec(shape, lambda bi, i: (0, 0))
    const3 = lambda shape: pl.BlockSpec(shape, lambda bi, i: (0, 0, 0))
    head_spec = pl.BlockSpec((1, HEADS, TM_PROJ, LANES), lambda bi, i: (bi, 0, i, 0))
    row_spec = lambda w: pl.BlockSpec((1, TM_PROJ, w), lambda bi, i: (bi, i, 0))
    hshape = jax.ShapeDtypeStruct((b, HEADS, s, LANES), _bf16)
    qp, kp, vtp, ga, ob = pl.pallas_call(
        _proj_kernel,
        grid=(b, s // TM_PROJ),
        in_specs=[row_spec(D_MODEL),
                  pl.BlockSpec((1, TM_PROJ // POS_PER_ROW, LANES), lambda bi, i: (bi, i, 0)),
                  const2((8, LANES)),
                  const2((D_MODEL, C_END)), const2((1, Q_LORA)), const2((Q_LORA, HP)),
                  const2((1, KV_LORA)), const2((KV_LORA, HP)), const2((HP, KV_LORA)),
                  const2((1, G_WIDTH)), const2((1, G_WIDTH)),
                  const3((G_HEADS // 2, CHUNK, 2 * CHUNK)), const2((CHUNK, G_WIDTH))],
        out_specs=[head_spec, head_spec,
                   pl.BlockSpec((1, HEADS, LANES, TM_PROJ), lambda bi, i: (bi, 0, 0, i)),
                   row_spec(A_WIDTH), row_spec(G_WIDTH)],
        out_shape=[hshape, hshape, jax.ShapeDtypeStruct((b, HEADS, LANES, s), _bf16),
                   jax.ShapeDtypeStruct((b, s, A_WIDTH), _bf16),
                   jax.ShapeDtypeStruct((b, s, G_WIDTH), _bf16)],
        scratch_shapes=[pltpu.VMEM((TM_PROJ, LANES), _f32)] * 3,
        compiler_params=pltpu.CompilerParams(
            dimension_semantics=("arbitrary", "arbitrary"), vmem_limit_bytes=VMEM_LIMIT),
        name="hyb_proj",
    )(x, pos8, aux, win_p, q_norm_g.reshape(1, -1), wq, kv_norm_g.reshape(1, -1), wk, wvt,
      sgu_norm_g.reshape(1, -1), sgu_norm_b.reshape(1, -1), wsp, bsp)

    hps = HEADS_PER_STEP
    per_pair = pl.BlockSpec((1, hps, s, LANES), lambda bi, hp: (bi, hp, 0, 0))
    gate_spec = pl.BlockSpec((1, s, LANES), lambda bi, hp: (bi, 0, hp))
    oa = pl.pallas_call(
        _attn_kernel,
        grid=(b, HEADS // hps),
        in_specs=[per_pair, per_pair,
                  pl.BlockSpec((1, hps, LANES, s), lambda bi, hp: (bi, hp, 0, 0)), gate_spec],
        out_specs=gate_spec,
        out_shape=jax.ShapeDtypeStruct((b, s, A_WIDTH), _bf16),
        scratch_shapes=[pltpu.VMEM((hps, 2, TK, TQ), _f32), pltpu.VMEM((hps, 2, 1, TQ), _f32),
                        pltpu.VMEM((hps, 1, TQ), _f32), pltpu.VMEM((hps, LANES, TQ), _f32)],
        compiler_params=pltpu.CompilerParams(
            dimension_semantics=("arbitrary", "arbitrary"), vmem_limit_bytes=VMEM_LIMIT_ATTN),
        name="hyb_attn",
    )(qp, kp, vtp, ga)

    r = b * s
    rows = lambda w: pl.BlockSpec((TM_OUT, w), lambda i: (i, 0))
    fixed = lambda shape: pl.BlockSpec(shape, lambda i: (0, 0))
    out = pl.pallas_call(
        _out_kernel,
        grid=(r // TM_OUT,),
        in_specs=[rows(D_MODEL), rows(A_WIDTH), rows(G_WIDTH), fixed((D_MODEL, D_MODEL)),
                  fixed((1, D_MODEL)), fixed((1, D_MODEL))],
        out_specs=rows(D_MODEL),
        out_shape=jax.ShapeDtypeStruct((r, D_MODEL), x.dtype),
        compiler_params=pltpu.CompilerParams(
            dimension_semantics=("arbitrary",), vmem_limit_bytes=VMEM_LIMIT),
        name="hyb_out",
    )(x.reshape(r, d), oa.reshape(r, A_WIDTH), ob.reshape(r, G_WIDTH), w_out.astype(_bf16),
      ln_g.reshape(1, -1), ln_b.reshape(1, -1))
    return out.reshape(b, s, d)
```

```python
import math

import jax
import jax.numpy as jnp
import numpy as np
from jax import lax
from jax.experimental import pallas as pl
from jax.experimental.pallas import tpu as pltpu

D_MODEL = 1024
HEADS = 8
NOPE = 64
ROPE = 32
HALF = ROPE // 2
POS_PER_ROW = 8
V_DIM = 64
ONES_ROW = (V_DIM, 0)
A_WIDTH = HEADS * V_DIM
Q_LORA = 256
KV_LORA = 128
ROPE_THETA = 10000.0
CHUNK = 128
G_HEADS = 8
G_WIDTH = 512
G_HEAD_DIM = G_WIDTH // G_HEADS
DN_ALPHA = 2.0 ** 0.25
EPS = 1e-5
SM_SCALE = 1.0 / math.sqrt(NOPE + ROPE)
Q_SCALE = SM_SCALE * math.log2(math.e)

LANES = 128
HP = HEADS * LANES
ROPE_LO = NOPE

C_CQ = 0
C_CKV = C_CQ + Q_LORA
C_KR = C_CKV + KV_LORA
C_ZA = C_KR + LANES
C_U = C_ZA + A_WIDTH
C_V = C_U + G_WIDTH
C_ZB = C_V + G_WIDTH
C_END = C_ZB + G_WIDTH

TM_PROJ = 512
PROJ_CHUNK = 512
TM_OUT = 1024
OUT_CHUNK = 256
TQ = 1024
TK = 512
HEADS_PER_STEP = 2
NEG_BIG = -1e30

VMEM_LIMIT = 48 * 1024 * 1024
VMEM_LIMIT_ATTN = 52 * 1024 * 1024

_f32 = jnp.float32
_bf16 = jnp.bfloat16


def _dot(a, b):
    return jnp.dot(a, b, preferred_element_type=_f32)


def _dot_nt(a, b):
    return lax.dot_general(a, b, (((1,), (1,)), ((), ())), preferred_element_type=_f32)


def _gelu(x):
    return 0.5 * x * (1.0 + lax.erf(x * (1.0 / math.sqrt(2.0))))


def _silu(x):
    return x * (1.0 / (1.0 + jnp.exp(-x)))


def _proj_kernel(x_ref, pos_ref, aux_ref, win_ref, gq_ref, wuq_ref, gkv_ref, wuk_ref,
                 wuvt_ref, sg_ref, sb_ref, wsp_ref, bsp_ref,
                 q_out, k_out, vt_out, ga_out, ob_out, cos_sc, sup_sc, sdn_sc):
    tm = x_ref.shape[1]

    ang = pos_ref[0].astype(_f32) * aux_ref[0:1, :]
    cos8 = jnp.cos(ang)
    sin8 = jnp.sin(ang)
    lane = lax.broadcasted_iota(jnp.int32, cos8.shape, 1)
    in_lo = (lane >= ROPE_LO) & (lane < ROPE_LO + HALF)
    in_hi = (lane >= ROPE_LO + HALF) & (lane < ROPE_LO + ROPE)

    def place(t8, lane0, i):
        shift = (lane0 - HALF * i) % LANES
        return t8 if shift == 0 else pltpu.roll(t8, shift, 1)

    for i in range(POS_PER_ROW):
        rows = pl.ds(i, tm // POS_PER_ROW, stride=POS_PER_ROW)
        cos_sc[rows, :] = jnp.where(in_lo, place(cos8, ROPE_LO, i),
                                    jnp.where(in_hi, place(cos8, ROPE_LO + HALF, i), 1.0))
        sup_sc[rows, :] = jnp.where(in_hi, place(sin8, ROPE_LO + HALF, i), 0.0)
        sdn_sc[rows, :] = jnp.where(in_lo, -place(sin8, ROPE_LO, i), 0.0)

    def rms(c, g_ref):
        return c * lax.rsqrt(jnp.mean(c * c, axis=-1, keepdims=True) + EPS) * g_ref[...]

    row = lax.broadcasted_iota(jnp.int32, (CHUNK, 2 * CHUNK), 0)
    col = lax.broadcasted_iota(jnp.int32, (CHUNK, 2 * CHUNK), 1)
    tri = (col % CHUNK) <= row
    lo_half = lax.broadcasted_iota(jnp.int32, (CHUNK, LANES), 1) < G_HEAD_DIM
    vt_row = lax.broadcasted_iota(jnp.int32, (LANES, PROJ_CHUNK), 0)
    ones_row = [vt_row == ONES_ROW[0], vt_row == ONES_ROW[1]]

    for rc in range(tm // PROJ_CHUNK):
        rs = slice(rc * PROJ_CHUNK, (rc + 1) * PROJ_CHUNK)
        x = x_ref[0, rs, :].astype(_bf16)

        def proj(lo, hi, x=x):
            return _dot(x, win_ref[:, lo:hi])

        def rope(t, rs=rs):
            return (t * cos_sc[rs, :] + pltpu.roll(t, HALF, 1) * sup_sc[rs, :]
                    + pltpu.roll(t, LANES - HALF, 1) * sdn_sc[rs, :])

        lat = proj(C_CQ, C_ZA)
        v_pre = proj(C_V, C_ZB)
        u_pre = proj(C_U, C_V)
        cqn = rms(lat[:, C_CQ:C_CKV], gq_ref).astype(_bf16)
        q = _dot(cqn, wuq_ref[...]) * Q_SCALE
        ckn = rms(lat[:, C_CKV:C_KR], gkv_ref).astype(_bf16)
        kn = _dot(ckn, wuk_ref[...])
        vt = _dot_nt(wuvt_ref[...], ckn)
        kr = rope(lat[:, C_KR:C_ZA])
        for h in range(HEADS):
            sl = slice(h * LANES, (h + 1) * LANES)
            q_out[0, h, rs, :] = rope(q[:, sl]).astype(_bf16)
            k_out[0, h, rs, :] = (kn[:, sl] + kr).astype(_bf16)
            vt_out[0, h, :, rs] = jnp.where(ones_row[h % 2], 1.0, vt[sl, :]).astype(_bf16)

        zb_pre = proj(C_ZB, C_END)
        ga_out[0, rs, :] = _silu(proj(C_ZA, C_U)).astype(_bf16)

        u = _gelu(u_pre)
        v = _gelu(v_pre)
        mu = jnp.mean(v, axis=-1, keepdims=True)
        vc = v - mu
        var = jnp.mean(vc * vc, axis=-1, keepdims=True)
        vn = (vc * lax.rsqrt(var + EPS) * sg_ref[...] + sb_ref[...])
        gb = _silu(zb_pre)

        for pair in range(G_HEADS // 2):
            w_pair = jnp.where(tri, wsp_ref[pair], 0.0).astype(_bf16)
            ls = slice(pair * LANES, (pair + 1) * LANES)
            n_chunks = PROJ_CHUNK // CHUNK
            stacked = []
            for c in range(n_chunks):
                vp = vn[c * CHUNK:(c + 1) * CHUNK, ls]
                stacked.append(jnp.concatenate(
                    [jnp.where(lo_half, vp, 0.0), jnp.where(lo_half, 0.0, vp)], axis=0
                ).astype(_bf16))
            sv_all = _dot(w_pair, jnp.concatenate(stacked, axis=1))
            for c in range(n_chunks):
                cs = slice(c * CHUNK, (c + 1) * CHUNK)
                os = slice(rc * PROJ_CHUNK + c * CHUNK, rc * PROJ_CHUNK + (c + 1) * CHUNK)
                sv = sv_all[:, c * LANES:(c + 1) * LANES] + bsp_ref[:, ls]
                ob_out[0, os, ls] = (u[cs, ls] * sv * gb[cs, ls]).astype(_bf16)


def _attn_kernel(q_ref, k_ref, vt_ref, g_ref, o_ref, s_sc, cmax_sc, m_sc, acc_sc):
    seq = q_ref.shape[2]
    nq = seq // TQ
    low = TK

    def causal(s):
        r = lax.broadcasted_iota(jnp.int32, s.shape, 0)
        c = lax.broadcasted_iota(jnp.int32, s.shape, 1)
        return jnp.where(r <= c, s, NEG_BIG)

    def scores(hh, slot, q0, j, c0=0, c1=TQ, masked=False):
        ks = pl.multiple_of(j * TK, TK)
        k = k_ref[0, hh, pl.ds(ks, TK), :]
        q = q_ref[0, hh, pl.ds(pl.multiple_of(q0 + c0, TK), c1 - c0), :]
        s = _dot_nt(k, q)
        if masked:
            s = causal(s)
        s_sc[hh, slot, :, c0:c1] = s
        cmax_sc[hh, slot, :, c0:c1] = jnp.max(s, axis=0, keepdims=True)

    def softmax_pv(hh, slot, j, c0=0, c1=TQ, mask_here=False):
        ks = pl.multiple_of(j * TK, TK)
        s = s_sc[hh, slot, :, c0:c1]
        if mask_here:
            s = causal(s)
            m_cur = jnp.max(s, axis=0, keepdims=True)
        else:
            m_cur = cmax_sc[hh, slot, :, c0:c1]
        m_prev = m_sc[hh, :, c0:c1]
        m_next = jnp.maximum(m_prev, m_cur)
        p = jnp.exp2(s - m_next).astype(_bf16)
        alpha = jnp.exp2(m_prev - m_next)
        vt = vt_ref[0, hh, :, pl.ds(ks, TK)]
        acc_sc[hh, :, c0:c1] = alpha * acc_sc[hh, :, c0:c1] + _dot(vt, p)
        m_sc[hh, :, c0:c1] = m_next

    scores(0, 0, 0, 0)

    def q_tile(qi, carry):
        q0 = pl.multiple_of(qi * TQ, TQ)
        m_sc[...] = jnp.full(m_sc.shape, NEG_BIG, _f32)
        acc_sc[...] = jnp.zeros(acc_sc.shape, _f32)

        def step(nxt, cur):
            for c0, c1 in ((0, low), (low, TQ)):
                scores(nxt[0], nxt[1], q0, nxt[2], c0, c1)
                softmax_pv(cur[0], cur[1], cur[2], c0, c1)

        def two_blocks(j):
            step((1, 0, j), (0, 0, j))
            step((0, 1, j + 1), (1, 0, j))
            step((1, 1, j + 1), (0, 1, j + 1))
            step((0, 0, j + 2), (1, 1, j + 1))

        def four_blocks(i, c):
            two_blocks(4 * i)
            two_blocks(4 * i + 2)
            return c

        def last_two_blocks(i, c):
            two_blocks(4 * (qi // 2))
            return c

        lax.fori_loop(0, qi // 2, four_blocks, 0)
        lax.fori_loop(0, qi % 2, last_two_blocks, 0)
        j0 = (TQ // TK) * qi
        scores(1, 0, q0, j0, masked=True)
        softmax_pv(0, 0, j0, mask_here=True)
        scores(0, 1, q0, j0 + 1, low, TQ, masked=True)
        softmax_pv(1, 0, j0)
        scores(1, 1, q0, j0 + 1, low, TQ, masked=True)
        softmax_pv(0, 1, j0 + 1, low, TQ)
        scores(0, 0, jnp.minimum(qi + 1, nq - 1) * TQ, 0)
        softmax_pv(1, 1, j0 + 1, low, TQ)

        acc0, acc1 = acc_sc[0], acc_sc[1]
        o0 = acc0 * (1.0 / acc0[ONES_ROW[0]:ONES_ROW[0] + 1, :])
        o1 = acc1 * (1.0 / acc1[ONES_ROW[1]:ONES_ROW[1] + 1, :])
        row = lax.broadcasted_iota(jnp.int32, (LANES, TQ), 0)
        both = jnp.where(row < V_DIM, o0, o1).T
        rows = pl.ds(q0, TQ)
        o_ref[0, rows, :] = (both * g_ref[0, rows, :].astype(_f32)).astype(_bf16)
        return carry

    lax.fori_loop(0, nq, q_tile, 0)


def _out_kernel(x_ref, oa_ref, ob_ref, wo_ref, g_ref, b_ref, o_ref):
    for c in range(TM_OUT // OUT_CHUNK):
        rs = slice(c * OUT_CHUNK, (c + 1) * OUT_CHUNK)
        y = _dot(oa_ref[rs, :], wo_ref[:A_WIDTH, :]) + _dot(ob_ref[rs, :], wo_ref[A_WIDTH:, :])
        z = DN_ALPHA * x_ref[rs, :] + y
        mu = jnp.mean(z, axis=-1, keepdims=True)
        zc = z - mu
        var = jnp.mean(zc * zc, axis=-1, keepdims=True)
        o_ref[rs, :] = zc * lax.rsqrt(var + EPS) * g_ref[...] + b_ref[...]


def _prep_weights(w_in, w_uq, w_ukv, w_spatial, b_spatial):
    kr0 = Q_LORA + KV_LORA
    zeros = lambda n: jnp.zeros((D_MODEL, n), w_in.dtype)
    win_p = jnp.concatenate([w_in[:, :kr0], zeros(ROPE_LO), w_in[:, kr0:kr0 + ROPE],
                             zeros(LANES - ROPE_LO - ROPE), w_in[:, kr0 + ROPE:]],
                            axis=1).astype(_bf16)

    wq = w_uq.reshape(Q_LORA, HEADS, NOPE + ROPE)
    wq = jnp.pad(wq, ((0, 0), (0, 0), (0, LANES - NOPE - ROPE))).reshape(Q_LORA, HP)
    wkv = w_ukv.reshape(KV_LORA, HEADS, NOPE + V_DIM)
    wk = jnp.pad(wkv[:, :, :NOPE], ((0, 0), (0, 0), (0, LANES - NOPE))).reshape(KV_LORA, HP)
    wv = wkv[:, :, NOPE:]
    wv_even = jnp.pad(wv[:, 0::2], ((0, 0), (0, 0), (0, LANES - V_DIM)))
    wv_odd = jnp.pad(wv[:, 1::2], ((0, 0), (0, 0), (LANES - V_DIM, 0)))
    wvt = jnp.stack([wv_even, wv_odd], axis=2).reshape(KV_LORA, HP).T

    wsp = w_spatial.reshape(G_HEADS // 2, 2, CHUNK, CHUNK).transpose(0, 2, 1, 3)
    wsp = wsp.reshape(G_HEADS // 2, CHUNK, 2 * CHUNK)
    bsp = jnp.repeat(b_spatial.T, G_HEAD_DIM, axis=1)
    return win_p, wq.astype(_bf16), wk.astype(_bf16), wvt.astype(_bf16), wsp, bsp


def _aux_table():
    inv_freq = 1.0 / (ROPE_THETA ** (jnp.arange(HALF, dtype=_f32) / HALF))
    freq = jnp.tile(inv_freq, POS_PER_ROW)
    return jnp.concatenate([freq[None], jnp.zeros((7, LANES), _f32)], axis=0)


def _pack_positions(positions):
    b, s = positions.shape
    return jnp.repeat(positions.reshape(b, s // POS_PER_ROW, POS_PER_ROW), HALF, axis=-1)


def kernel(x, positions, w_in, q_norm_g, w_uq, kv_norm_g, w_ukv, sgu_norm_g, sgu_norm_b,
           w_spatial, b_spatial, w_out, ln_g, ln_b):
    b, s, d = x.shape
    assert d == D_MODEL and s % TQ == 0 and s % TM_PROJ == 0 and TQ == 2 * TK and HEADS_PER_STEP == 2
    win_p, wq, wk, wvt, wsp, bsp = _prep_weights(w_in, w_uq, w_ukv, w_spatial, b_spatial)
    aux = _aux_table()
    pos8 = _pack_positions(positions)

    const2 = lambda shape: pl.BlockSpec(shape, lambda bi, i: (0, 0))
    const3 = lambda shape: pl.BlockSpec(shape, lambda bi, i: (0, 0, 0))
    head_spec = pl.BlockSpec((1, HEADS, TM_PROJ, LANES), lambda bi, i: (bi, 0, i, 0))
    row_spec = lambda w: pl.BlockSpec((1, TM_PROJ, w), lambda bi, i: (bi, i, 0))
    hshape = jax.ShapeDtypeStruct((b, HEADS, s, LANES), _bf16)
    qp, kp, vtp, ga, ob = pl.pallas_call(
        _proj_kernel,
        grid=(b, s // TM_PROJ),
        in_specs=[row_spec(D_MODEL),
                  pl.BlockSpec((1, TM_PROJ // POS_PER_ROW, LANES), lambda bi, i: (bi, i, 0)),
                  const2((8, LANES)),
                  const2((D_MODEL, C_END)), const2((1, Q_LORA)), const2((Q_LORA, HP)),
                  const2((1, KV_LORA)), const2((KV_LORA, HP)), const2((HP, KV_LORA)),
                  const2((1, G_WIDTH)), const2((1, G_WIDTH)),
                  const3((G_HEADS // 2, CHUNK, 2 * CHUNK)), const2((CHUNK, G_WIDTH))],
        out_specs=[head_spec, head_spec,
                   pl.BlockSpec((1, HEADS, LANES, TM_PROJ), lambda bi, i: (bi, 0, 0, i)),
                   row_spec(A_WIDTH), row_spec(G_WIDTH)],
        out_shape=[hshape, hshape, jax.ShapeDtypeStruct((b, HEADS, LANES, s), _bf16),
                   jax.ShapeDtypeStruct((b, s, A_WIDTH), _bf16),
                   jax.ShapeDtypeStruct((b, s, G_WIDTH), _bf16)],
        scratch_shapes=[pltpu.VMEM((TM_PROJ, LANES), _f32)] * 3,
        compiler_params=pltpu.CompilerParams(
            dimension_semantics=("arbitrary", "arbitrary"), vmem_limit_bytes=VMEM_LIMIT),
        name="hyb_proj",
    )(x, pos8, aux, win_p, q_norm_g.reshape(1, -1), wq, kv_norm_g.reshape(1, -1), wk, wvt,
      sgu_norm_g.reshape(1, -1), sgu_norm_b.reshape(1, -1), wsp, bsp)

    hps = HEADS_PER_STEP
    per_pair = pl.BlockSpec((1, hps, s, LANES), lambda bi, hp: (bi, hp, 0, 0))
    gate_spec = pl.BlockSpec((1, s, LANES), lambda bi, hp: (bi, 0, hp))
    oa = pl.pallas_call(
        _attn_kernel,
        grid=(b, HEADS // hps),
        in_specs=[per_pair, per_pair,
                  pl.BlockSpec((1, hps, LANES, s), lambda bi, hp: (bi, hp, 0, 0)), gate_spec],
        out_specs=gate_spec,
        out_shape=jax.ShapeDtypeStruct((b, s, A_WIDTH), _bf16),
        scratch_shapes=[pltpu.VMEM((hps, 2, TK, TQ), _f32), pltpu.VMEM((hps, 2, 1, TQ), _f32),
                        pltpu.VMEM((hps, 1, TQ), _f32), pltpu.VMEM((hps, LANES, TQ), _f32)],
        compiler_params=pltpu.CompilerParams(
            dimension_semantics=("arbitrary", "arbitrary"), vmem_limit_bytes=VMEM_LIMIT_ATTN),
        name="hyb_attn",
    )(qp, kp, vtp, ga)

    r = b * s
    rows = lambda w: pl.BlockSpec((TM_OUT, w), lambda i: (i, 0))
    fixed = lambda shape: pl.BlockSpec(shape, lambda i: (0, 0))
    out = pl.pallas_call(
        _out_kernel,
        grid=(r // TM_OUT,),
        in_specs=[rows(D_MODEL), rows(A_WIDTH), rows(G_WIDTH), fixed((D_MODEL, D_MODEL)),
                  fixed((1, D_MODEL)), fixed((1, D_MODEL))],
        out_specs=rows(D_MODEL),
        out_shape=jax.ShapeDtypeStruct((r, D_MODEL), x.dtype),
        compiler_params=pltpu.CompilerParams(
            dimension_semantics=("arbitrary",), vmem_limit_bytes=VMEM_LIMIT),
        name="hyb_out",
    )(x.reshape(r, d), oa.reshape(r, A_WIDTH), ob.reshape(r, G_WIDTH), w_out.astype(_bf16),
      ln_g.reshape(1, -1), ln_b.reshape(1, -1))
    return out.reshape(b, s, d)
```

```python
import math

import jax
import jax.numpy as jnp
import numpy as np
from jax import lax
from jax.experimental import pallas as pl
from jax.experimental.pallas import tpu as pltpu

D_MODEL = 1024
HEADS = 8
NOPE = 64
ROPE = 32
HALF = ROPE // 2
POS_PER_ROW = 8
V_DIM = 64
ONES_ROW = (V_DIM, 0)
A_WIDTH = HEADS * V_DIM
Q_LORA = 256
KV_LORA = 128
ROPE_THETA = 10000.0
CHUNK = 128
G_HEADS = 8
G_WIDTH = 512
G_HEAD_DIM = G_WIDTH // G_HEADS
DN_ALPHA = 2.0 ** 0.25
EPS = 1e-5
SM_SCALE = 1.0 / math.sqrt(NOPE + ROPE)
Q_SCALE = SM_SCALE * math.log2(math.e)

LANES = 128
HP = HEADS * LANES
ROPE_LO = NOPE

C_CQ = 0
C_CKV = C_CQ + Q_LORA
C_KR = C_CKV + KV_LORA
C_ZA = C_KR + LANES
C_U = C_ZA + A_WIDTH
C_V = C_U + G_WIDTH
C_ZB = C_V + G_WIDTH
C_END = C_ZB + G_WIDTH

TM_PROJ = 512
PROJ_CHUNK = 512
TM_OUT = 1024
OUT_CHUNK = 256
TQ = 1024
TK = 512
HEADS_PER_STEP = 2
NEG_BIG = -1e30

VMEM_LIMIT = 48 * 1024 * 1024
VMEM_LIMIT_ATTN = 52 * 1024 * 1024

_f32 = jnp.float32
_bf16 = jnp.bfloat16


def _dot(a, b):
    return jnp.dot(a, b, preferred_element_type=_f32)


def _dot_nt(a, b):
    return lax.dot_general(a, b, (((1,), (1,)), ((), ())), preferred_element_type=_f32)


def _gelu(x):
    return 0.5 * x * (1.0 + lax.erf(x * (1.0 / math.sqrt(2.0))))


def _silu(x):
    return x * (1.0 / (1.0 + jnp.exp(-x)))


def _proj_kernel(x_ref, pos_ref, aux_ref, win_ref, gq_ref, wuq_ref, gkv_ref, wuk_ref,
                 wuvt_ref, sg_ref, sb_ref, wsp_ref, bsp_ref,
                 q_out, k_out, vt_out, ga_out, ob_out, cos_sc, sup_sc, sdn_sc):
    tm = x_ref.shape[1]

    ang = pos_ref[0].astype(_f32) * aux_ref[0:1, :]
    cos8 = jnp.cos(ang)
    sin8 = jnp.sin(ang)
    lane = lax.broadcasted_iota(jnp.int32, cos8.shape, 1)
    in_lo = (lane >= ROPE_LO) & (lane < ROPE_LO + HALF)
    in_hi = (lane >= ROPE_LO + HALF) & (lane < ROPE_LO + ROPE)

    def place(t8, lane0, i):
        shift = (lane0 - HALF * i) % LANES
        return t8 if shift == 0 else pltpu.roll(t8, shift, 1)

    for i in range(POS_PER_ROW):
        rows = pl.ds(i, tm // POS_PER_ROW, stride=POS_PER_ROW)
        cos_sc[rows, :] = jnp.where(in_lo, place(cos8, ROPE_LO, i),
                                    jnp.where(in_hi, place(cos8, ROPE_LO + HALF, i), 1.0))
        sup_sc[rows, :] = jnp.where(in_hi, place(sin8, ROPE_LO + HALF, i), 0.0)
        sdn_sc[rows, :] = jnp.where(in_lo, -place(sin8, ROPE_LO, i), 0.0)

    def rms(c, g_ref):
        return c * lax.rsqrt(jnp.mean(c * c, axis=-1, keepdims=True) + EPS) * g_ref[...]

    row = lax.broadcasted_iota(jnp.int32, (CHUNK, 2 * CHUNK), 0)
    col = lax.broadcasted_iota(jnp.int32, (CHUNK, 2 * CHUNK), 1)
    tri = (col % CHUNK) <= row
    lo_half = lax.broadcasted_iota(jnp.int32, (CHUNK, LANES), 1) < G_HEAD_DIM
    vt_row = lax.broadcasted_iota(jnp.int32, (LANES, PROJ_CHUNK), 0)
    ones_row = [vt_row == ONES_ROW[0], vt_row == ONES_ROW[1]]

    for rc in range(tm // PROJ_CHUNK):
        rs = slice(rc * PROJ_CHUNK, (rc + 1) * PROJ_CHUNK)
        x = x_ref[0, rs, :].astype(_bf16)

        def proj(lo, hi, x=x):
            return _dot(x, win_ref[:, lo:hi])

        def rope(t, rs=rs):
            return (t * cos_sc[rs, :] + pltpu.roll(t, HALF, 1) * sup_sc[rs, :]
                    + pltpu.roll(t, LANES - HALF, 1) * sdn_sc[rs, :])

        lat = proj(C_CQ, C_ZA)
        v_pre = proj(C_V, C_ZB)
        u_pre = proj(C_U, C_V)
        cqn = rms(lat[:, C_CQ:C_CKV], gq_ref).astype(_bf16)
        q = _dot(cqn, wuq_ref[...]) * Q_SCALE
        ckn = rms(lat[:, C_CKV:C_KR], gkv_ref).astype(_bf16)
        kn = _dot(ckn, wuk_ref[...])
        vt = _dot_nt(wuvt_ref[...], ckn)
        kr = rope(lat[:, C_KR:C_ZA])
        for h in range(HEADS):
            sl = slice(h * LANES, (h + 1) * LANES)
            q_out[0, h, rs, :] = rope(q[:, sl]).astype(_bf16)
            k_out[0, h, rs, :] = (kn[:, sl] + kr).astype(_bf16)
            vt_out[0, h, :, rs] = jnp.where(ones_row[h % 2], 1.0, vt[sl, :]).astype(_bf16)

        zb_pre = proj(C_ZB, C_END)
        ga_out[0, rs, :] = _silu(proj(C_ZA, C_U)).astype(_bf16)

        u = _gelu(u_pre)
        v = _gelu(v_pre)
        mu = jnp.mean(v, axis=-1, keepdims=True)
        vc = v - mu
        var = jnp.mean(vc * vc, axis=-1, keepdims=True)
        vn = (vc * lax.rsqrt(var + EPS) * sg_ref[...] + sb_ref[...])
        gb = _silu(zb_pre)

        for pair in range(G_HEADS // 2):
            w_pair = jnp.where(tri, wsp_ref[pair], 0.0).astype(_bf16)
            ls = slice(pair * LANES, (pair + 1) * LANES)
            n_chunks = PROJ_CHUNK // CHUNK
            stacked = []
            for c in range(n_chunks):
                vp = vn[c * CHUNK:(c + 1) * CHUNK, ls]
                stacked.append(jnp.concatenate(
                    [jnp.where(lo_half, vp, 0.0), jnp.where(lo_half, 0.0, vp)], axis=0
                ).astype(_bf16))
            sv_all = _dot(w_pair, jnp.concatenate(stacked, axis=1))
            for c in range(n_chunks):
                cs = slice(c * CHUNK, (c + 1) * CHUNK)
                os = slice(rc * PROJ_CHUNK + c * CHUNK, rc * PROJ_CHUNK + (c + 1) * CHUNK)
                sv = sv_all[:, c * LANES:(c + 1) * LANES] + bsp_ref[:, ls]
                ob_out[0, os, ls] = (u[cs, ls] * sv * gb[cs, ls]).astype(_bf16)


def _attn_kernel(q_ref, k_ref, vt_ref, g_ref, o_ref, s_sc, cmax_sc, m_sc, acc_sc):
    seq = q_ref.shape[2]
    nq = seq // TQ
    low = TK

    def causal(s):
        r = lax.broadcasted_iota(jnp.int32, s.shape, 0)
        c = lax.broadcasted_iota(jnp.int32, s.shape, 1)
        return jnp.where(r <= c, s, NEG_BIG)

    def scores(hh, slot, q0, j, c0=0, c1=TQ, masked=False):
        ks = pl.multiple_of(j * TK, TK)
        k = k_ref[0, hh, pl.ds(ks, TK), :]
        q = q_ref[0, hh, pl.ds(pl.multiple_of(q0 + c0, TK), c1 - c0), :]
        s = _dot_nt(k, q)
        if masked:
            s = causal(s)
        s_sc[hh, slot, :, c0:c1] = s
        cmax_sc[hh, slot, :, c0:c1] = jnp.max(s, axis=0, keepdims=True)

    def softmax_pv(hh, slot, j, c0=0, c1=TQ, mask_here=False):
        ks = pl.multiple_of(j * TK, TK)
        s = s_sc[hh, slot, :, c0:c1]
        if mask_here:
            s = causal(s)
            m_cur = jnp.max(s, axis=0, keepdims=True)
        else:
            m_cur = cmax_sc[hh, slot, :, c0:c1]
        m_prev = m_sc[hh, :, c0:c1]
        m_next = jnp.maximum(m_prev, m_cur)
        p = jnp.exp2(s - m_next).astype(_bf16)
        alpha = jnp.exp2(m_prev - m_next)
        vt = vt_ref[0, hh, :, pl.ds(ks, TK)]
        acc_sc[hh, :, c0:c1] = alpha * acc_sc[hh, :, c0:c1] + _dot(vt, p)
        m_sc[hh, :, c0:c1] = m_next

    scores(0, 0, 0, 0)

    def q_tile(qi, carry):
        q0 = pl.multiple_of(qi * TQ, TQ)
        m_sc[...] = jnp.full(m_sc.shape, NEG_BIG, _f32)
        acc_sc[...] = jnp.zeros(acc_sc.shape, _f32)

        def step(nxt, cur):
            for c0, c1 in ((0, 256), (256, 512), (512, 768), (768, 1024)):
                scores(nxt[0], nxt[1], q0, nxt[2], c0, c1)
                softmax_pv(cur[0], cur[1], cur[2], c0, c1)

        def two_blocks(j):
            step((1, 0, j), (0, 0, j))
            step((0, 1, j + 1), (1, 0, j))
            step((1, 1, j + 1), (0, 1, j + 1))
            step((0, 0, j + 2), (1, 1, j + 1))

        def four_blocks(i, c):
            two_blocks(4 * i)
            two_blocks(4 * i + 2)
            return c

        def last_two_blocks(i, c):
            two_blocks(4 * (qi // 2))
            return c

        lax.fori_loop(0, qi // 2, four_blocks, 0)
        lax.fori_loop(0, qi % 2, last_two_blocks, 0)
        j0 = (TQ // TK) * qi
        scores(1, 0, q0, j0, masked=True)
        softmax_pv(0, 0, j0, mask_here=True)
        scores(0, 1, q0, j0 + 1, low, TQ, masked=True)
        softmax_pv(1, 0, j0)
        scores(1, 1, q0, j0 + 1, low, TQ, masked=True)
        softmax_pv(0, 1, j0 + 1, low, TQ)
        scores(0, 0, jnp.minimum(qi + 1, nq - 1) * TQ, 0)
        softmax_pv(1, 1, j0 + 1, low, TQ)

        acc0, acc1 = acc_sc[0], acc_sc[1]
        o0 = acc0 * (1.0 / acc0[ONES_ROW[0]:ONES_ROW[0] + 1, :])
        o1 = acc1 * (1.0 / acc1[ONES_ROW[1]:ONES_ROW[1] + 1, :])
        row = lax.broadcasted_iota(jnp.int32, (LANES, TQ), 0)
        both = jnp.where(row < V_DIM, o0, o1).T
        rows = pl.ds(q0, TQ)
        o_ref[0, rows, :] = (both * g_ref[0, rows, :].astype(_f32)).astype(_bf16)
        return carry

    lax.fori_loop(0, nq, q_tile, 0)


def _out_kernel(x_ref, oa_ref, ob_ref, wo_ref, g_ref, b_ref, o_ref):
    for c in range(TM_OUT // OUT_CHUNK):
        rs = slice(c * OUT_CHUNK, (c + 1) * OUT_CHUNK)
        y = _dot(oa_ref[rs, :], wo_ref[:A_WIDTH, :]) + _dot(ob_ref[rs, :], wo_ref[A_WIDTH:, :])
        z = DN_ALPHA * x_ref[rs, :] + y
        mu = jnp.mean(z, axis=-1, keepdims=True)
        zc = z - mu
        var = jnp.mean(zc * zc, axis=-1, keepdims=True)
        o_ref[rs, :] = zc * lax.rsqrt(var + EPS) * g_ref[...] + b_ref[...]


def _prep_weights(w_in, w_uq, w_ukv, w_spatial, b_spatial):
    kr0 = Q_LORA + KV_LORA
    zeros = lambda n: jnp.zeros((D_MODEL, n), w_in.dtype)
    win_p = jnp.concatenate([w_in[:, :kr0], zeros(ROPE_LO), w_in[:, kr0:kr0 + ROPE],
                             zeros(LANES - ROPE_LO - ROPE), w_in[:, kr0 + ROPE:]],
                            axis=1).astype(_bf16)

    wq = w_uq.reshape(Q_LORA, HEADS, NOPE + ROPE)
    wq = jnp.pad(wq, ((0, 0), (0, 0), (0, LANES - NOPE - ROPE))).reshape(Q_LORA, HP)
    wkv = w_ukv.reshape(KV_LORA, HEADS, NOPE + V_DIM)
    wk = jnp.pad(wkv[:, :, :NOPE], ((0, 0), (0, 0), (0, LANES - NOPE))).reshape(KV_LORA, HP)
    wv = wkv[:, :, NOPE:]
    wv_even = jnp.pad(wv[:, 0::2], ((0, 0), (0, 0), (0, LANES - V_DIM)))
    wv_odd = jnp.pad(wv[:, 1::2], ((0, 0), (0, 0), (LANES - V_DIM, 0)))
    wvt = jnp.stack([wv_even, wv_odd], axis=2).reshape(KV_LORA, HP).T

    wsp = w_spatial.reshape(G_HEADS // 2, 2, CHUNK, CHUNK).transpose(0, 2, 1, 3)
    wsp = wsp.reshape(G_HEADS // 2, CHUNK, 2 * CHUNK)
    bsp = jnp.repeat(b_spatial.T, G_HEAD_DIM, axis=1)
    return win_p, wq.astype(_bf16), wk.astype(_bf16), wvt.astype(_bf16), wsp, bsp


def _aux_table():
    inv_freq = 1.0 / (ROPE_THETA ** (jnp.arange(HALF, dtype=_f32) / HALF))
    freq = jnp.tile(inv_freq, POS_PER_ROW)
    return jnp.concatenate([freq[None], jnp.zeros((7, LANES), _f32)], axis=0)


def _pack_positions(positions):
    b, s = positions.shape
    return jnp.repeat(positions.reshape(b, s // POS_PER_ROW, POS_PER_ROW), HALF, axis=-1)


def kernel(x, positions, w_in, q_norm_g, w_uq, kv_norm_g, w_ukv, sgu_norm_g, sgu_norm_b,
           w_spatial, b_spatial, w_out, ln_g, ln_b):
    b, s, d = x.shape
    assert d == D_MODEL and s % TQ == 0 and s % TM_PROJ == 0 and TQ == 2 * TK and HEADS_PER_STEP == 2
    win_p, wq, wk, wvt, wsp, bsp = _prep_weights(w_in, w_uq, w_ukv, w_spatial, b_spatial)
    aux = _aux_table()
    pos8 = _pack_positions(positions)

    const2 = lambda shape: pl.BlockSpec(shape, lambda bi, i: (0, 0))
    const3 = lambda shape: pl.BlockSpec(shape, lambda bi, i: (0, 0, 0))
    head_spec = pl.BlockSpec((1, HEADS, TM_PROJ, LANES), lambda bi, i: (bi, 0, i, 0))
    row_spec = lambda w: pl.BlockSpec((1, TM_PROJ, w), lambda bi, i: (bi, i, 0))
    hshape = jax.ShapeDtypeStruct((b, HEADS, s, LANES), _bf16)
    qp, kp, vtp, ga, ob = pl.pallas_call(
        _proj_kernel,
        grid=(b, s // TM_PROJ),
        in_specs=[row_spec(D_MODEL),
                  pl.BlockSpec((1, TM_PROJ // POS_PER_ROW, LANES), lambda bi, i: (bi, i, 0)),
                  const2((8, LANES)),
                  const2((D_MODEL, C_END)), const2((1, Q_LORA)), const2((Q_LORA, HP)),
                  const2((1, KV_LORA)), const2((KV_LORA, HP)), const2((HP, KV_LORA)),
                  const2((1, G_WIDTH)), const2((1, G_WIDTH)),
                  const3((G_HEADS // 2, CHUNK, 2 * CHUNK)), const2((CHUNK, G_WIDTH))],
        out_specs=[head_spec, head_spec,
                   pl.BlockSpec((1, HEADS, LANES, TM_PROJ), lambda bi, i: (bi, 0, 0, i)),
                   row_spec(A_WIDTH), row_spec(G_WIDTH)],
        out_shape=[hshape, hshape, jax.ShapeDtypeStruct((b, HEADS, LANES, s), _bf16),
                   jax.ShapeDtypeStruct((b, s, A_WIDTH), _bf16),
                   jax.ShapeDtypeStruct((b, s, G_WIDTH), _bf16)],
        scratch_shapes=[pltpu.VMEM((TM_PROJ, LANES), _f32)] * 3,
        compiler_params=pltpu.CompilerParams(
            dimension_semantics=("arbitrary", "arbitrary"), vmem_limit_bytes=VMEM_LIMIT),
        name="hyb_proj",
    )(x, pos8, aux, win_p, q_norm_g.reshape(1, -1), wq, kv_norm_g.reshape(1, -1), wk, wvt,
      sgu_norm_g.reshape(1, -1), sgu_norm_b.reshape(1, -1), wsp, bsp)

    hps = HEADS_PER_STEP
    per_pair = pl.BlockSpec((1, hps, s, LANES), lambda bi, hp: (bi, hp, 0, 0))
    gate_spec = pl.BlockSpec((1, s, LANES), lambda bi, hp: (bi, 0, hp))
    oa = pl.pallas_call(
        _attn_kernel,
        grid=(b, HEADS // hps),
        in_specs=[per_pair, per_pair,
                  pl.BlockSpec((1, hps, LANES, s), lambda bi, hp: (bi, hp, 0, 0)), gate_spec],
        out_specs=gate_spec,
        out_shape=jax.ShapeDtypeStruct((b, s, A_WIDTH), _bf16),
        scratch_shapes=[pltpu.VMEM((hps, 2, TK, TQ), _f32), pltpu.VMEM((hps, 2, 1, TQ), _f32),
                        pltpu.VMEM((hps, 1, TQ), _f32), pltpu.VMEM((hps, LANES, TQ), _f32)],
        compiler_params=pltpu.CompilerParams(
            dimension_semantics=("arbitrary", "arbitrary"), vmem_limit_bytes=VMEM_LIMIT_ATTN),
        name="hyb_attn",
    )(qp, kp, vtp, ga)

    r = b * s
    rows = lambda w: pl.BlockSpec((TM_OUT, w), lambda i: (i, 0))
    fixed = lambda shape: pl.BlockSpec(shape, lambda i: (0, 0))
    out = pl.pallas_call(
        _out_kernel,
        grid=(r // TM_OUT,),
        in_specs=[rows(D_MODEL), rows(A_WIDTH), rows(G_WIDTH), fixed((D_MODEL, D_MODEL)),
                  fixed((1, D_MODEL)), fixed((1, D_MODEL))],
        out_specs=rows(D_MODEL),
        out_shape=jax.ShapeDtypeStruct((r, D_MODEL), x.dtype),
        compiler_params=pltpu.CompilerParams(
            dimension_semantics=("arbitrary",), vmem_limit_bytes=VMEM_LIMIT),
        name="hyb_out",
    )(x.reshape(r, d), oa.reshape(r, A_WIDTH), ob.reshape(r, G_WIDTH), w_out.astype(_bf16),
      ln_g.reshape(1, -1), ln_b.reshape(1, -1))
    return out.reshape(b, s, d)
```

```python
import math

import jax
import jax.numpy as jnp
import numpy as np
from jax import lax
from jax.experimental import pallas as pl
from jax.experimental.pallas import tpu as pltpu

D_MODEL = 1024
HEADS = 8
NOPE = 64
ROPE = 32
HALF = ROPE // 2
POS_PER_ROW = 8
V_DIM = 64
ONES_ROW = (V_DIM, 0)
A_WIDTH = HEADS * V_DIM
Q_LORA = 256
KV_LORA = 128
ROPE_THETA = 10000.0
CHUNK = 128
G_HEADS = 8
G_WIDTH = 512
G_HEAD_DIM = G_WIDTH // G_HEADS
DN_ALPHA = 2.0 ** 0.25
EPS = 1e-5
SM_SCALE = 1.0 / math.sqrt(NOPE + ROPE)
Q_SCALE = SM_SCALE * math.log2(math.e)

LANES = 128
HP = HEADS * LANES
ROPE_LO = NOPE

C_CQ = 0
C_CKV = C_CQ + Q_LORA
C_KR = C_CKV + KV_LORA
C_ZA = C_KR + LANES
C_U = C_ZA + A_WIDTH
C_V = C_U + G_WIDTH
C_ZB = C_V + G_WIDTH
C_END = C_ZB + G_WIDTH

TM_PROJ = 512
PROJ_CHUNK = 512
TM_OUT = 1024
OUT_CHUNK = 256
TQ = 1024
TK = 512
QW = 256
HEADS_PER_STEP = 2
NEG_BIG = -1e30

VMEM_LIMIT = 48 * 1024 * 1024
VMEM_LIMIT_ATTN = 52 * 1024 * 1024

_f32 = jnp.float32
_bf16 = jnp.bfloat16


def _dot(a, b):
    return jnp.dot(a, b, preferred_element_type=_f32)


def _dot_nt(a, b):
    return lax.dot_general(a, b, (((1,), (1,)), ((), ())), preferred_element_type=_f32)


def _gelu(x):
    return 0.5 * x * (1.0 + lax.erf(x * (1.0 / math.sqrt(2.0))))


def _silu(x):
    return x * (1.0 / (1.0 + jnp.exp(-x)))


def _proj_kernel(x_ref, pos_ref, aux_ref, win_ref, gq_ref, wuq_ref, gkv_ref, wuk_ref,
                 wuvt_ref, sg_ref, sb_ref, wsp_ref, bsp_ref,
                 q_out, k_out, vt_out, ga_out, ob_out, cos_sc, sup_sc, sdn_sc):
    tm = x_ref.shape[1]

    ang = pos_ref[0].astype(_f32) * aux_ref[0:1, :]
    cos8 = jnp.cos(ang)
    sin8 = jnp.sin(ang)
    lane = lax.broadcasted_iota(jnp.int32, cos8.shape, 1)
    in_lo = (lane >= ROPE_LO) & (lane < ROPE_LO + HALF)
    in_hi = (lane >= ROPE_LO + HALF) & (lane < ROPE_LO + ROPE)

    def place(t8, lane0, i):
        shift = (lane0 - HALF * i) % LANES
        return t8 if shift == 0 else pltpu.roll(t8, shift, 1)

    for i in range(POS_PER_ROW):
        rows = pl.ds(i, tm // POS_PER_ROW, stride=POS_PER_ROW)
        cos_sc[rows, :] = jnp.where(in_lo, place(cos8, ROPE_LO, i),
                                    jnp.where(in_hi, place(cos8, ROPE_LO + HALF, i), 1.0))
        sup_sc[rows, :] = jnp.where(in_hi, place(sin8, ROPE_LO + HALF, i), 0.0)
        sdn_sc[rows, :] = jnp.where(in_lo, -place(sin8, ROPE_LO, i), 0.0)

    def rms(c, g_ref):
        return c * lax.rsqrt(jnp.mean(c * c, axis=-1, keepdims=True) + EPS) * g_ref[...]

    row = lax.broadcasted_iota(jnp.int32, (CHUNK, 2 * CHUNK), 0)
    col = lax.broadcasted_iota(jnp.int32, (CHUNK, 2 * CHUNK), 1)
    tri = (col % CHUNK) <= row
    lo_half = lax.broadcasted_iota(jnp.int32, (CHUNK, LANES), 1) < G_HEAD_DIM
    vt_row = lax.broadcasted_iota(jnp.int32, (LANES, PROJ_CHUNK), 0)
    ones_row = [vt_row == ONES_ROW[0], vt_row == ONES_ROW[1]]

    for rc in range(tm // PROJ_CHUNK):
        rs = slice(rc * PROJ_CHUNK, (rc + 1) * PROJ_CHUNK)
        x = x_ref[0, rs, :].astype(_bf16)

        def proj(lo, hi, x=x):
            return _dot(x, win_ref[:, lo:hi])

        def rope(t, rs=rs):
            return (t * cos_sc[rs, :] + pltpu.roll(t, HALF, 1) * sup_sc[rs, :]
                    + pltpu.roll(t, LANES - HALF, 1) * sdn_sc[rs, :])

        lat = proj(C_CQ, C_ZA)
        v_pre = proj(C_V, C_ZB)
        u_pre = proj(C_U, C_V)
        cqn = rms(lat[:, C_CQ:C_CKV], gq_ref).astype(_bf16)
        q = _dot(cqn, wuq_ref[...]) * Q_SCALE
        ckn = rms(lat[:, C_CKV:C_KR], gkv_ref).astype(_bf16)
        kn = _dot(ckn, wuk_ref[...])
        vt = _dot_nt(wuvt_ref[...], ckn)
        kr = rope(lat[:, C_KR:C_ZA])
        for h in range(HEADS):
            sl = slice(h * LANES, (h + 1) * LANES)
            q_out[0, h, rs, :] = rope(q[:, sl]).astype(_bf16)
            k_out[0, h, rs, :] = (kn[:, sl] + kr).astype(_bf16)
            vt_out[0, h, :, rs] = jnp.where(ones_row[h % 2], 1.0, vt[sl, :]).astype(_bf16)

        zb_pre = proj(C_ZB, C_END)
        ga_out[0, rs, :] = _silu(proj(C_ZA, C_U)).astype(_bf16)

        u = _gelu(u_pre)
        v = _gelu(v_pre)
        mu = jnp.mean(v, axis=-1, keepdims=True)
        vc = v - mu
        var = jnp.mean(vc * vc, axis=-1, keepdims=True)
        vn = (vc * lax.rsqrt(var + EPS) * sg_ref[...] + sb_ref[...])
        gb = _silu(zb_pre)

        for pair in range(G_HEADS // 2):
            w_pair = jnp.where(tri, wsp_ref[pair], 0.0).astype(_bf16)
            ls = slice(pair * LANES, (pair + 1) * LANES)
            n_chunks = PROJ_CHUNK // CHUNK
            stacked = []
            for c in range(n_chunks):
                vp = vn[c * CHUNK:(c + 1) * CHUNK, ls]
                stacked.append(jnp.concatenate(
                    [jnp.where(lo_half, vp, 0.0), jnp.where(lo_half, 0.0, vp)], axis=0
                ).astype(_bf16))
            sv_all = _dot(w_pair, jnp.concatenate(stacked, axis=1))
            for c in range(n_chunks):
                cs = slice(c * CHUNK, (c + 1) * CHUNK)
                os = slice(rc * PROJ_CHUNK + c * CHUNK, rc * PROJ_CHUNK + (c + 1) * CHUNK)
                sv = sv_all[:, c * LANES:(c + 1) * LANES] + bsp_ref[:, ls]
                ob_out[0, os, ls] = (u[cs, ls] * sv * gb[cs, ls]).astype(_bf16)


def _attn_kernel(q_ref, k_ref, vt_ref, g_ref, o_ref, s_sc, cmax_sc, m_sc, acc_sc):
    seq = q_ref.shape[2]
    nq = seq // TQ

    def causal(s, t):
        r = lax.broadcasted_iota(jnp.int32, s.shape, 0)
        c = lax.broadcasted_iota(jnp.int32, s.shape, 1)
        return jnp.where(r - c <= t, s, NEG_BIG)

    def scores(hh, slot, q0, j, c0=0, c1=TQ, r0=0, r1=TK, koff=None):
        ks = pl.multiple_of(j * TK + r0, QW)
        k = k_ref[0, hh, pl.ds(ks, r1 - r0), :]
        q = q_ref[0, hh, pl.ds(pl.multiple_of(q0 + c0, QW), c1 - c0), :]
        s = _dot_nt(k, q)
        if koff is not None:
            s = causal(s, c0 - r0 - koff)
        s_sc[hh, slot, r0:r1, c0:c1] = s
        cmax_sc[hh, slot, :, c0:c1] = jnp.max(s, axis=0, keepdims=True)

    def softmax_pv(hh, slot, j, c0=0, c1=TQ, r0=0, r1=TK, koff_here=None):
        ks = pl.multiple_of(j * TK + r0, QW)
        s = s_sc[hh, slot, r0:r1, c0:c1]
        if koff_here is not None:
            s = causal(s, c0 - r0 - koff_here)
            m_cur = jnp.max(s, axis=0, keepdims=True)
        else:
            m_cur = cmax_sc[hh, slot, :, c0:c1]
        m_prev = m_sc[hh, :, c0:c1]
        m_next = jnp.maximum(m_prev, m_cur)
        p = jnp.exp2(s - m_next).astype(_bf16)
        alpha = jnp.exp2(m_prev - m_next)
        vt = vt_ref[0, hh, :, pl.ds(ks, r1 - r0)]
        acc_sc[hh, :, c0:c1] = alpha * acc_sc[hh, :, c0:c1] + _dot(vt, p)
        m_sc[hh, :, c0:c1] = m_next

    def diag_quarters(koff):
        out = []
        for c0 in range(0, TQ, QW):
            last_key = c0 + QW - 1 - koff
            if last_key < 0:
                continue
            r1 = min(TK, (last_key // QW + 1) * QW)
            out.append((c0, c0 + QW, r1, c0 - koff < r1 - 1))
        return out

    scores(0, 0, 0, 0)

    def q_tile(qi, carry):
        q0 = pl.multiple_of(qi * TQ, TQ)
        m_sc[...] = jnp.full(m_sc.shape, NEG_BIG, _f32)
        acc_sc[...] = jnp.zeros(acc_sc.shape, _f32)

        def step(nxt, cur):
            for c0 in range(0, TQ, QW):
                scores(nxt[0], nxt[1], q0, nxt[2], c0, c0 + QW)
                softmax_pv(cur[0], cur[1], cur[2], c0, c0 + QW)

        def two_blocks(j):
            step((1, 0, j), (0, 0, j))
            step((0, 1, j + 1), (1, 0, j))
            step((1, 1, j + 1), (0, 1, j + 1))
            step((0, 0, j + 2), (1, 1, j + 1))

        def four_blocks(i, c):
            two_blocks(4 * i)
            two_blocks(4 * i + 2)
            return c

        def last_two_blocks(i, c):
            two_blocks(4 * (qi // 2))
            return c

        lax.fori_loop(0, qi // 2, four_blocks, 0)
        lax.fori_loop(0, qi % 2, last_two_blocks, 0)
        j0 = (TQ // TK) * qi
        q_next = jnp.minimum(qi + 1, nq - 1) * TQ
        full = [(c0, c0 + QW, TK, False) for c0 in range(0, TQ, QW)]
        units = [dict(hh=0, slot=0, q0=q0, j=j0, koff=0, parts=diag_quarters(0), stored_masked=False),
                 dict(hh=1, slot=0, q0=q0, j=j0, koff=0, parts=diag_quarters(0), stored_masked=True),
                 dict(hh=0, slot=1, q0=q0, j=j0 + 1, koff=TK, parts=diag_quarters(TK), stored_masked=True),
                 dict(hh=1, slot=1, q0=q0, j=j0 + 1, koff=TK, parts=diag_quarters(TK), stored_masked=True),
                 dict(hh=0, slot=0, q0=q_next, j=0, koff=None, parts=full, stored_masked=True)]
        for cur, nxt in zip(units[:-1], units[1:]):
            for i in range(max(len(cur["parts"]), len(nxt["parts"]))):
                if i < len(nxt["parts"]):
                    c0, c1, r1, masked = nxt["parts"][i]
                    scores(nxt["hh"], nxt["slot"], nxt["q0"], nxt["j"], c0, c1, 0, r1,
                           nxt["koff"] if masked else None)
                if i < len(cur["parts"]):
                    c0, c1, r1, masked = cur["parts"][i]
                    mask_now = masked and not cur["stored_masked"]
                    softmax_pv(cur["hh"], cur["slot"], cur["j"], c0, c1, 0, r1,
                               cur["koff"] if mask_now else None)

        acc0, acc1 = acc_sc[0], acc_sc[1]
        o0 = acc0 * (1.0 / acc0[ONES_ROW[0]:ONES_ROW[0] + 1, :])
        o1 = acc1 * (1.0 / acc1[ONES_ROW[1]:ONES_ROW[1] + 1, :])
        row = lax.broadcasted_iota(jnp.int32, (LANES, TQ), 0)
        both = jnp.where(row < V_DIM, o0, o1).T
        rows = pl.ds(q0, TQ)
        o_ref[0, rows, :] = (both * g_ref[0, rows, :].astype(_f32)).astype(_bf16)
        return carry

    lax.fori_loop(0, nq, q_tile, 0)


def _out_kernel(x_ref, oa_ref, ob_ref, wo_ref, g_ref, b_ref, o_ref):
    for c in range(TM_OUT // OUT_CHUNK):
        rs = slice(c * OUT_CHUNK, (c + 1) * OUT_CHUNK)
        y = _dot(oa_ref[rs, :], wo_ref[:A_WIDTH, :]) + _dot(ob_ref[rs, :], wo_ref[A_WIDTH:, :])
        z = DN_ALPHA * x_ref[rs, :] + y
        mu = jnp.mean(z, axis=-1, keepdims=True)
        zc = z - mu
        var = jnp.mean(zc * zc, axis=-1, keepdims=True)
        o_ref[rs, :] = zc * lax.rsqrt(var + EPS) * g_ref[...] + b_ref[...]


def _prep_weights(w_in, w_uq, w_ukv, w_spatial, b_spatial):
    kr0 = Q_LORA + KV_LORA
    zeros = lambda n: jnp.zeros((D_MODEL, n), w_in.dtype)
    win_p = jnp.concatenate([w_in[:, :kr0], zeros(ROPE_LO), w_in[:, kr0:kr0 + ROPE],
                             zeros(LANES - ROPE_LO - ROPE), w_in[:, kr0 + ROPE:]],
                            axis=1).astype(_bf16)

    wq = w_uq.reshape(Q_LORA, HEADS, NOPE + ROPE)
    wq = jnp.pad(wq, ((0, 0), (0, 0), (0, LANES - NOPE - ROPE))).reshape(Q_LORA, HP)
    wkv = w_ukv.reshape(KV_LORA, HEADS, NOPE + V_DIM)
    wk = jnp.pad(wkv[:, :, :NOPE], ((0, 0), (0, 0), (0, LANES - NOPE))).reshape(KV_LORA, HP)
    wv = wkv[:, :, NOPE:]
    wv_even = jnp.pad(wv[:, 0::2], ((0, 0), (0, 0), (0, LANES - V_DIM)))
    wv_odd = jnp.pad(wv[:, 1::2], ((0, 0), (0, 0), (LANES - V_DIM, 0)))
    wvt = jnp.stack([wv_even, wv_odd], axis=2).reshape(KV_LORA, HP).T

    wsp = w_spatial.reshape(G_HEADS // 2, 2, CHUNK, CHUNK).transpose(0, 2, 1, 3)
    wsp = wsp.reshape(G_HEADS // 2, CHUNK, 2 * CHUNK)
    bsp = jnp.repeat(b_spatial.T, G_HEAD_DIM, axis=1)
    return win_p, wq.astype(_bf16), wk.astype(_bf16), wvt.astype(_bf16), wsp, bsp


def _aux_table():
    inv_freq = 1.0 / (ROPE_THETA ** (jnp.arange(HALF, dtype=_f32) / HALF))
    freq = jnp.tile(inv_freq, POS_PER_ROW)
    return jnp.concatenate([freq[None], jnp.zeros((7, LANES), _f32)], axis=0)


def _pack_positions(positions):
    b, s = positions.shape
    return jnp.repeat(positions.reshape(b, s // POS_PER_ROW, POS_PER_ROW), HALF, axis=-1)


def kernel(x, positions, w_in, q_norm_g, w_uq, kv_norm_g, w_ukv, sgu_norm_g, sgu_norm_b,
           w_spatial, b_spatial, w_out, ln_g, ln_b):
    b, s, d = x.shape
    assert d == D_MODEL and s % TQ == 0 and s % TM_PROJ == 0 and TQ == 2 * TK and HEADS_PER_STEP == 2
    win_p, wq, wk, wvt, wsp, bsp = _prep_weights(w_in, w_uq, w_ukv, w_spatial, b_spatial)
    aux = _aux_table()
    pos8 = _pack_positions(positions)

    const2 = lambda shape: pl.BlockSpec(shape, lambda bi, i: (0, 0))
    const3 = lambda shape: pl.BlockSpec(shape, lambda bi, i: (0, 0, 0))
    head_spec = pl.BlockSpec((1, HEADS, TM_PROJ, LANES), lambda bi, i: (bi, 0, i, 0))
    row_spec = lambda w: pl.BlockSpec((1, TM_PROJ, w), lambda bi, i: (bi, i, 0))
    hshape = jax.ShapeDtypeStruct((b, HEADS, s, LANES), _bf16)
    qp, kp, vtp, ga, ob = pl.pallas_call(
        _proj_kernel,
        grid=(b, s // TM_PROJ),
        in_specs=[row_spec(D_MODEL),
                  pl.BlockSpec((1, TM_PROJ // POS_PER_ROW, LANES), lambda bi, i: (bi, i, 0)),
                  const2((8, LANES)),
                  const2((D_MODEL, C_END)), const2((1, Q_LORA)), const2((Q_LORA, HP)),
                  const2((1, KV_LORA)), const2((KV_LORA, HP)), const2((HP, KV_LORA)),
                  const2((1, G_WIDTH)), const2((1, G_WIDTH)),
                  const3((G_HEADS // 2, CHUNK, 2 * CHUNK)), const2((CHUNK, G_WIDTH))],
        out_specs=[head_spec, head_spec,
                   pl.BlockSpec((1, HEADS, LANES, TM_PROJ), lambda bi, i: (bi, 0, 0, i)),
                   row_spec(A_WIDTH), row_spec(G_WIDTH)],
        out_shape=[hshape, hshape, jax.ShapeDtypeStruct((b, HEADS, LANES, s), _bf16),
                   jax.ShapeDtypeStruct((b, s, A_WIDTH), _bf16),
                   jax.ShapeDtypeStruct((b, s, G_WIDTH), _bf16)],
        scratch_shapes=[pltpu.VMEM((TM_PROJ, LANES), _f32)] * 3,
        compiler_params=pltpu.CompilerParams(
            dimension_semantics=("arbitrary", "arbitrary"), vmem_limit_bytes=VMEM_LIMIT),
        name="hyb_proj",
    )(x, pos8, aux, win_p, q_norm_g.reshape(1, -1), wq, kv_norm_g.reshape(1, -1), wk, wvt,
      sgu_norm_g.reshape(1, -1), sgu_norm_b.reshape(1, -1), wsp, bsp)

    hps = HEADS_PER_STEP
    per_pair = pl.BlockSpec((1, hps, s, LANES), lambda bi, hp: (bi, hp, 0, 0))
    gate_spec = pl.BlockSpec((1, s, LANES), lambda bi, hp: (bi, 0, hp))
    oa = pl.pallas_call(
        _attn_kernel,
        grid=(b, HEADS // hps),
        in_specs=[per_pair, per_pair,
                  pl.BlockSpec((1, hps, LANES, s), lambda bi, hp: (bi, hp, 0, 0)), gate_spec],
        out_specs=gate_spec,
        out_shape=jax.ShapeDtypeStruct((b, s, A_WIDTH), _bf16),
        scratch_shapes=[pltpu.VMEM((hps, 2, TK, TQ), _f32), pltpu.VMEM((hps, 2, 1, TQ), _f32),
                        pltpu.VMEM((hps, 1, TQ), _f32), pltpu.VMEM((hps, LANES, TQ), _f32)],
        compiler_params=pltpu.CompilerParams(
            dimension_semantics=("arbitrary", "arbitrary"), vmem_limit_bytes=VMEM_LIMIT_ATTN),
        name="hyb_attn",
    )(qp, kp, vtp, ga)

    r = b * s
    rows = lambda w: pl.BlockSpec((TM_OUT, w), lambda i: (i, 0))
    fixed = lambda shape: pl.BlockSpec(shape, lambda i: (0, 0))
    out = pl.pallas_call(
        _out_kernel,
        grid=(r // TM_OUT,),
        in_specs=[rows(D_MODEL), rows(A_WIDTH), rows(G_WIDTH), fixed((D_MODEL, D_MODEL)),
                  fixed((1, D_MODEL)), fixed((1, D_MODEL))],
        out_specs=rows(D_MODEL),
        out_shape=jax.ShapeDtypeStruct((r, D_MODEL), x.dtype),
        compiler_params=pltpu.CompilerParams(
            dimension_semantics=("arbitrary",), vmem_limit_bytes=VMEM_LIMIT),
        name="hyb_out",
    )(x.reshape(r, d), oa.reshape(r, A_WIDTH), ob.reshape(r, G_WIDTH), w_out.astype(_bf16),
      ln_g.reshape(1, -1), ln_b.reshape(1, -1))
    return out.reshape(b, s, d)
```

```python
import math

import jax
import jax.numpy as jnp
import numpy as np
from jax import lax
from jax.experimental import pallas as pl
from jax.experimental.pallas import tpu as pltpu

D_MODEL = 1024
HEADS = 8
NOPE = 64
ROPE = 32
HALF = ROPE // 2
POS_PER_ROW = 8
V_DIM = 64
ONES_ROW = (V_DIM, 0)
A_WIDTH = HEADS * V_DIM
Q_LORA = 256
KV_LORA = 128
ROPE_THETA = 10000.0
CHUNK = 128
G_HEADS = 8
G_WIDTH = 512
G_HEAD_DIM = G_WIDTH // G_HEADS
DN_ALPHA = 2.0 ** 0.25
EPS = 1e-5
SM_SCALE = 1.0 / math.sqrt(NOPE + ROPE)
Q_SCALE = SM_SCALE * math.log2(math.e)

LANES = 128
HP = HEADS * LANES
ROPE_LO = NOPE

C_CQ = 0
C_CKV = C_CQ + Q_LORA
C_KR = C_CKV + KV_LORA
C_ZA = C_KR + LANES
C_U = C_ZA + A_WIDTH
C_V = C_U + G_WIDTH
C_ZB = C_V + G_WIDTH
C_END = C_ZB + G_WIDTH

TM_PROJ = 512
PROJ_CHUNK = 512
TM_OUT = 2048
OUT_CHUNK = 256
TQ = 1024
TK = 512
QW = 256
HEADS_PER_STEP = 2
NEG_BIG = -1e30

VMEM_LIMIT = 48 * 1024 * 1024
VMEM_LIMIT_ATTN = 52 * 1024 * 1024

_f32 = jnp.float32
_bf16 = jnp.bfloat16


def _dot(a, b):
    return jnp.dot(a, b, preferred_element_type=_f32)


def _dot_nt(a, b):
    return lax.dot_general(a, b, (((1,), (1,)), ((), ())), preferred_element_type=_f32)


def _gelu(x):
    return 0.5 * x * (1.0 + lax.erf(x * (1.0 / math.sqrt(2.0))))


def _silu(x):
    return x * (1.0 / (1.0 + jnp.exp(-x)))


def _proj_kernel(x_ref, pos_ref, aux_ref, win_ref, gq_ref, wuq_ref, gkv_ref, wuk_ref,
                 wuvt_ref, sg_ref, sb_ref, wsp_ref, bsp_ref,
                 q_out, k_out, vt_out, ga_out, ob_out, cos_sc, sup_sc, sdn_sc):
    tm = x_ref.shape[1]

    ang = pos_ref[0].astype(_f32) * aux_ref[0:1, :]
    cos8 = jnp.cos(ang)
    sin8 = jnp.sin(ang)
    lane = lax.broadcasted_iota(jnp.int32, cos8.shape, 1)
    in_lo = (lane >= ROPE_LO) & (lane < ROPE_LO + HALF)
    in_hi = (lane >= ROPE_LO + HALF) & (lane < ROPE_LO + ROPE)

    def place(t8, lane0, i):
        shift = (lane0 - HALF * i) % LANES
        return t8 if shift == 0 else pltpu.roll(t8, shift, 1)

    for i in range(POS_PER_ROW):
        rows = pl.ds(i, tm // POS_PER_ROW, stride=POS_PER_ROW)
        cos_sc[rows, :] = jnp.where(in_lo, place(cos8, ROPE_LO, i),
                                    jnp.where(in_hi, place(cos8, ROPE_LO + HALF, i), 1.0))
        sup_sc[rows, :] = jnp.where(in_hi, place(sin8, ROPE_LO + HALF, i), 0.0)
        sdn_sc[rows, :] = jnp.where(in_lo, -place(sin8, ROPE_LO, i), 0.0)

    def rms(c, g_ref):
        return c * lax.rsqrt(jnp.mean(c * c, axis=-1, keepdims=True) + EPS) * g_ref[...]

    row = lax.broadcasted_iota(jnp.int32, (CHUNK, 2 * CHUNK), 0)
    col = lax.broadcasted_iota(jnp.int32, (CHUNK, 2 * CHUNK), 1)
    tri = (col % CHUNK) <= row
    lo_half = lax.broadcasted_iota(jnp.int32, (CHUNK, LANES), 1) < G_HEAD_DIM
    vt_row = lax.broadcasted_iota(jnp.int32, (LANES, PROJ_CHUNK), 0)
    own_rows = [vt_row < V_DIM, vt_row >= V_DIM]
    nope_lanes = lax.broadcasted_iota(jnp.int32, (PROJ_CHUNK, LANES), 1) < NOPE
    fill_rows = [jnp.where(vt_row == ONES_ROW[par], 1.0, 0.0) for par in range(2)]

    for rc in range(tm // PROJ_CHUNK):
        rs = slice(rc * PROJ_CHUNK, (rc + 1) * PROJ_CHUNK)
        x = x_ref[0, rs, :].astype(_bf16)

        def proj(lo, hi, x=x):
            return _dot(x, win_ref[:, lo:hi])

        def rope(t, rs=rs):
            return (t * cos_sc[rs, :] + pltpu.roll(t, HALF, 1) * sup_sc[rs, :]
                    + pltpu.roll(t, LANES - HALF, 1) * sdn_sc[rs, :])

        lat = proj(C_CQ, C_ZA)
        v_pre = proj(C_V, C_ZB)
        u_pre = proj(C_U, C_V)
        cqn = rms(lat[:, C_CQ:C_CKV], gq_ref).astype(_bf16)
        q = _dot(cqn, wuq_ref[...]) * Q_SCALE
        ckn = rms(lat[:, C_CKV:C_KR], gkv_ref).astype(_bf16)
        kn = _dot(ckn, wuk_ref[...])
        vt = _dot_nt(wuvt_ref[...], ckn)
        kr = rope(lat[:, C_KR:C_ZA])
        for h in range(HEADS):
            sl = slice(h * LANES, (h + 1) * LANES)
            q_out[0, h, rs, :] = rope(q[:, sl]).astype(_bf16)
            kn_pair = kn[:, (h // 2) * LANES:(h // 2 + 1) * LANES]
            kn_h = kn_pair if h % 2 == 0 else pltpu.roll(kn_pair, NOPE, 1)
            k_out[0, h, rs, :] = (jnp.where(nope_lanes, kn_h, 0.0) + kr).astype(_bf16)
            vt_pair = vt[(h // 2) * LANES:(h // 2 + 1) * LANES, :]
            vt_out[0, h, :, rs] = jnp.where(own_rows[h % 2], vt_pair, fill_rows[h % 2]).astype(_bf16)

        zb_pre = proj(C_ZB, C_END)
        ga_out[0, rs, :] = _silu(proj(C_ZA, C_U)).astype(_bf16)

        u = _gelu(u_pre)
        v = _gelu(v_pre)
        mu = jnp.mean(v, axis=-1, keepdims=True)
        vc = v - mu
        var = jnp.mean(vc * vc, axis=-1, keepdims=True)
        vn = (vc * lax.rsqrt(var + EPS) * sg_ref[...] + sb_ref[...])
        gb = _silu(zb_pre)

        for pair in range(G_HEADS // 2):
            w_pair = jnp.where(tri, wsp_ref[pair], 0.0).astype(_bf16)
            ls = slice(pair * LANES, (pair + 1) * LANES)
            n_chunks = PROJ_CHUNK // CHUNK
            stacked = []
            for c in range(n_chunks):
                vp = vn[c * CHUNK:(c + 1) * CHUNK, ls]
                stacked.append(jnp.concatenate(
                    [jnp.where(lo_half, vp, 0.0), jnp.where(lo_half, 0.0, vp)], axis=0
                ).astype(_bf16))
            sv_all = _dot(w_pair, jnp.concatenate(stacked, axis=1))
            for c in range(n_chunks):
                cs = slice(c * CHUNK, (c + 1) * CHUNK)
                os = slice(rc * PROJ_CHUNK + c * CHUNK, rc * PROJ_CHUNK + (c + 1) * CHUNK)
                sv = sv_all[:, c * LANES:(c + 1) * LANES] + bsp_ref[:, ls]
                ob_out[0, os, ls] = (u[cs, ls] * sv * gb[cs, ls]).astype(_bf16)


def _attn_kernel(q_ref, k_ref, vt_ref, g_ref, o_ref, s_sc, cmax_sc, m_sc, acc_sc):
    seq = q_ref.shape[2]
    nq = seq // TQ

    def causal(s, t):
        r = lax.broadcasted_iota(jnp.int32, s.shape, 0)
        c = lax.broadcasted_iota(jnp.int32, s.shape, 1)
        return jnp.where(r - c <= t, s, NEG_BIG)

    def scores(hh, slot, q0, j, c0=0, c1=TQ, r0=0, r1=TK, koff=None):
        ks = pl.multiple_of(j * TK + r0, QW)
        k = k_ref[0, hh, pl.ds(ks, r1 - r0), :]
        q = q_ref[0, hh, pl.ds(pl.multiple_of(q0 + c0, QW), c1 - c0), :]
        s = _dot_nt(k, q)
        if koff is not None:
            s = causal(s, c0 - r0 - koff)
        s_sc[hh, slot, r0:r1, c0:c1] = s
        cmax_sc[hh, slot, :, c0:c1] = jnp.max(s, axis=0, keepdims=True)

    def softmax_pv(hh, slot, j, c0=0, c1=TQ, r0=0, r1=TK, koff_here=None):
        ks = pl.multiple_of(j * TK + r0, QW)
        s = s_sc[hh, slot, r0:r1, c0:c1]
        if koff_here is not None:
            s = causal(s, c0 - r0 - koff_here)
            m_cur = jnp.max(s, axis=0, keepdims=True)
        else:
            m_cur = cmax_sc[hh, slot, :, c0:c1]
        m_prev = m_sc[hh, :, c0:c1]
        m_next = jnp.maximum(m_prev, m_cur)
        p = jnp.exp2(s - m_next).astype(_bf16)
        alpha = jnp.exp2(m_prev - m_next)
        vt = vt_ref[0, hh, :, pl.ds(ks, r1 - r0)]
        acc_sc[hh, :, c0:c1] = alpha * acc_sc[hh, :, c0:c1] + _dot(vt, p)
        m_sc[hh, :, c0:c1] = m_next

    def diag_quarters(koff):
        out = []
        for c0 in range(0, TQ, QW):
            last_key = c0 + QW - 1 - koff
            if last_key < 0:
                continue
            r1 = min(TK, (last_key // QW + 1) * QW)
            out.append((c0, c0 + QW, r1, c0 - koff < r1 - 1))
        return out

    scores(0, 0, 0, 0)

    def q_tile(qi, carry):
        q0 = pl.multiple_of(qi * TQ, TQ)
        m_sc[...] = jnp.full(m_sc.shape, NEG_BIG, _f32)
        acc_sc[...] = jnp.zeros(acc_sc.shape, _f32)

        def step(nxt, cur):
            for c0 in range(0, TQ, QW):
                scores(nxt[0], nxt[1], q0, nxt[2], c0, c0 + QW)
                softmax_pv(cur[0], cur[1], cur[2], c0, c0 + QW)

        def two_blocks(j):
            step((1, 0, j), (0, 0, j))
            step((0, 1, j + 1), (1, 0, j))
            step((1, 1, j + 1), (0, 1, j + 1))
            step((0, 0, j + 2), (1, 1, j + 1))

        def four_blocks(i, c):
            two_blocks(4 * i)
            two_blocks(4 * i + 2)
            return c

        def last_two_blocks(i, c):
            two_blocks(4 * (qi // 2))
            return c

        lax.fori_loop(0, qi // 2, four_blocks, 0)
        lax.fori_loop(0, qi % 2, last_two_blocks, 0)
        j0 = (TQ // TK) * qi
        q_next = jnp.minimum(qi + 1, nq - 1) * TQ
        full = [(c0, c0 + QW, TK, False) for c0 in range(0, TQ, QW)]
        units = [dict(hh=0, slot=0, q0=q0, j=j0, koff=0, parts=diag_quarters(0), stored_masked=False),
                 dict(hh=1, slot=0, q0=q0, j=j0, koff=0, parts=diag_quarters(0), stored_masked=True),
                 dict(hh=0, slot=1, q0=q0, j=j0 + 1, koff=TK, parts=diag_quarters(TK), stored_masked=True),
                 dict(hh=1, slot=1, q0=q0, j=j0 + 1, koff=TK, parts=diag_quarters(TK), stored_masked=True),
                 dict(hh=0, slot=0, q0=q_next, j=0, koff=None, parts=full, stored_masked=True)]
        for cur, nxt in zip(units[:-1], units[1:]):
            for i in range(max(len(cur["parts"]), len(nxt["parts"]))):
                if i < len(nxt["parts"]):
                    c0, c1, r1, masked = nxt["parts"][i]
                    scores(nxt["hh"], nxt["slot"], nxt["q0"], nxt["j"], c0, c1, 0, r1,
                           nxt["koff"] if masked else None)
                if i < len(cur["parts"]):
                    c0, c1, r1, masked = cur["parts"][i]
                    mask_now = masked and not cur["stored_masked"]
                    softmax_pv(cur["hh"], cur["slot"], cur["j"], c0, c1, 0, r1,
                               cur["koff"] if mask_now else None)

        acc0, acc1 = acc_sc[0], acc_sc[1]
        o0 = acc0 * (1.0 / acc0[ONES_ROW[0]:ONES_ROW[0] + 1, :])
        o1 = acc1 * (1.0 / acc1[ONES_ROW[1]:ONES_ROW[1] + 1, :])
        row = lax.broadcasted_iota(jnp.int32, (LANES, TQ), 0)
        both = jnp.where(row < V_DIM, o0, o1).T
        rows = pl.ds(q0, TQ)
        o_ref[0, rows, :] = (both * g_ref[0, rows, :].astype(_f32)).astype(_bf16)
        return carry

    lax.fori_loop(0, nq, q_tile, 0)


def _out_kernel(x_ref, oa_ref, ob_ref, wo_ref, g_ref, b_ref, o_ref):
    for c in range(TM_OUT // OUT_CHUNK):
        rs = slice(c * OUT_CHUNK, (c + 1) * OUT_CHUNK)
        y = _dot(oa_ref[rs, :], wo_ref[:A_WIDTH, :]) + _dot(ob_ref[rs, :], wo_ref[A_WIDTH:, :])
        z = DN_ALPHA * x_ref[rs, :] + y
        mu = jnp.mean(z, axis=-1, keepdims=True)
        zc = z - mu
        var = jnp.mean(zc * zc, axis=-1, keepdims=True)
        o_ref[rs, :] = zc * lax.rsqrt(var + EPS) * g_ref[...] + b_ref[...]


def _prep_weights(w_in, w_uq, w_ukv, w_spatial, b_spatial):
    kr0 = Q_LORA + KV_LORA
    zeros = lambda n: jnp.zeros((D_MODEL, n), w_in.dtype)
    win_p = jnp.concatenate([w_in[:, :kr0], zeros(ROPE_LO), w_in[:, kr0:kr0 + ROPE],
                             zeros(LANES - ROPE_LO - ROPE), w_in[:, kr0 + ROPE:]],
                            axis=1).astype(_bf16)

    wq = w_uq.reshape(Q_LORA, HEADS, NOPE + ROPE)
    wq = jnp.pad(wq, ((0, 0), (0, 0), (0, LANES - NOPE - ROPE))).reshape(Q_LORA, HP)
    wkv = w_ukv.reshape(KV_LORA, HEADS, NOPE + V_DIM)
    wk = wkv[:, :, :NOPE].reshape(KV_LORA, HEADS * NOPE)
    wvt = wkv[:, :, NOPE:].reshape(KV_LORA, A_WIDTH).T

    wsp = w_spatial.reshape(G_HEADS // 2, 2, CHUNK, CHUNK).transpose(0, 2, 1, 3)
    wsp = wsp.reshape(G_HEADS // 2, CHUNK, 2 * CHUNK)
    bsp = jnp.repeat(b_spatial.T, G_HEAD_DIM, axis=1)
    return win_p, wq.astype(_bf16), wk.astype(_bf16), wvt.astype(_bf16), wsp, bsp


def _aux_table():
    inv_freq = 1.0 / (ROPE_THETA ** (jnp.arange(HALF, dtype=_f32) / HALF))
    freq = jnp.tile(inv_freq, POS_PER_ROW)
    return jnp.concatenate([freq[None], jnp.zeros((7, LANES), _f32)], axis=0)


def _pack_positions(positions):
    b, s = positions.shape
    return jnp.repeat(positions.reshape(b, s // POS_PER_ROW, POS_PER_ROW), HALF, axis=-1)


def kernel(x, positions, w_in, q_norm_g, w_uq, kv_norm_g, w_ukv, sgu_norm_g, sgu_norm_b,
           w_spatial, b_spatial, w_out, ln_g, ln_b):
    b, s, d = x.shape
    assert d == D_MODEL and s % TQ == 0 and s % TM_PROJ == 0 and TQ == 2 * TK and HEADS_PER_STEP == 2
    win_p, wq, wk, wvt, wsp, bsp = _prep_weights(w_in, w_uq, w_ukv, w_spatial, b_spatial)
    aux = _aux_table()
    pos8 = _pack_positions(positions)

    const2 = lambda shape: pl.BlockSpec(shape, lambda bi, i: (0, 0))
    const3 = lambda shape: pl.BlockSpec(shape, lambda bi, i: (0, 0, 0))
    head_spec = pl.BlockSpec((1, HEADS, TM_PROJ, LANES), lambda bi, i: (bi, 0, i, 0))
    row_spec = lambda w: pl.BlockSpec((1, TM_PROJ, w), lambda bi, i: (bi, i, 0))
    hshape = jax.ShapeDtypeStruct((b, HEADS, s, LANES), _bf16)
    qp, kp, vtp, ga, ob = pl.pallas_call(
        _proj_kernel,
        grid=(b, s // TM_PROJ),
        in_specs=[row_spec(D_MODEL),
                  pl.BlockSpec((1, TM_PROJ // POS_PER_ROW, LANES), lambda bi, i: (bi, i, 0)),
                  const2((8, LANES)),
                  const2((D_MODEL, C_END)), const2((1, Q_LORA)), const2((Q_LORA, HP)),
                  const2((1, KV_LORA)), const2((KV_LORA, HEADS * NOPE)), const2((A_WIDTH, KV_LORA)),
                  const2((1, G_WIDTH)), const2((1, G_WIDTH)),
                  const3((G_HEADS // 2, CHUNK, 2 * CHUNK)), const2((CHUNK, G_WIDTH))],
        out_specs=[head_spec, head_spec,
                   pl.BlockSpec((1, HEADS, LANES, TM_PROJ), lambda bi, i: (bi, 0, 0, i)),
                   row_spec(A_WIDTH), row_spec(G_WIDTH)],
        out_shape=[hshape, hshape, jax.ShapeDtypeStruct((b, HEADS, LANES, s), _bf16),
                   jax.ShapeDtypeStruct((b, s, A_WIDTH), _bf16),
                   jax.ShapeDtypeStruct((b, s, G_WIDTH), _bf16)],
        scratch_shapes=[pltpu.VMEM((TM_PROJ, LANES), _f32)] * 3,
        compiler_params=pltpu.CompilerParams(
            dimension_semantics=("arbitrary", "arbitrary"), vmem_limit_bytes=VMEM_LIMIT),
        name="hyb_proj",
    )(x, pos8, aux, win_p, q_norm_g.reshape(1, -1), wq, kv_norm_g.reshape(1, -1), wk, wvt,
      sgu_norm_g.reshape(1, -1), sgu_norm_b.reshape(1, -1), wsp, bsp)

    hps = HEADS_PER_STEP
    per_pair = pl.BlockSpec((1, hps, s, LANES), lambda bi, hp: (bi, hp, 0, 0))
    gate_spec = pl.BlockSpec((1, s, LANES), lambda bi, hp: (bi, 0, hp))
    oa = pl.pallas_call(
        _attn_kernel,
        grid=(b, HEADS // hps),
        in_specs=[per_pair, per_pair,
                  pl.BlockSpec((1, hps, LANES, s), lambda bi, hp: (bi, hp, 0, 0)), gate_spec],
        out_specs=gate_spec,
        out_shape=jax.ShapeDtypeStruct((b, s, A_WIDTH), _bf16),
        scratch_shapes=[pltpu.VMEM((hps, 2, TK, TQ), _f32), pltpu.VMEM((hps, 2, 1, TQ), _f32),
                        pltpu.VMEM((hps, 1, TQ), _f32), pltpu.VMEM((hps, LANES, TQ), _f32)],
        compiler_params=pltpu.CompilerParams(
            dimension_semantics=("arbitrary", "arbitrary"), vmem_limit_bytes=VMEM_LIMIT_ATTN),
        name="hyb_attn",
    )(qp, kp, vtp, ga)

    r = b * s
    rows = lambda w: pl.BlockSpec((TM_OUT, w), lambda i: (i, 0))
    fixed = lambda shape: pl.BlockSpec(shape, lambda i: (0, 0))
    out = pl.pallas_call(
        _out_kernel,
        grid=(r // TM_OUT,),
        in_specs=[rows(D_MODEL), rows(A_WIDTH), rows(G_WIDTH), fixed((D_MODEL, D_MODEL)),
                  fixed((1, D_MODEL)), fixed((1, D_MODEL))],
        out_specs=rows(D_MODEL),
        out_shape=jax.ShapeDtypeStruct((r, D_MODEL), x.dtype),
        compiler_params=pltpu.CompilerParams(
            dimension_semantics=("arbitrary",), vmem_limit_bytes=VMEM_LIMIT),
        name="hyb_out",
    )(x.reshape(r, d), oa.reshape(r, A_WIDTH), ob.reshape(r, G_WIDTH), w_out.astype(_bf16),
      ln_g.reshape(1, -1), ln_b.reshape(1, -1))
    return out.reshape(b, s, d)
```

```python
import math

import jax
import jax.numpy as jnp
import numpy as np
from jax import lax
from jax.experimental import pallas as pl
from jax.experimental.pallas import tpu as pltpu

D_MODEL = 1024
HEADS = 8
NOPE = 64
ROPE = 32
HALF = ROPE // 2
POS_PER_ROW = 8
V_DIM = 64
ONES_ROW = (V_DIM, 0)
A_WIDTH = HEADS * V_DIM
Q_LORA = 256
KV_LORA = 128
ROPE_THETA = 10000.0
CHUNK = 128
G_HEADS = 8
G_WIDTH = 512
G_HEAD_DIM = G_WIDTH // G_HEADS
DN_ALPHA = 2.0 ** 0.25
EPS = 1e-5
SM_SCALE = 1.0 / math.sqrt(NOPE + ROPE)
Q_SCALE = SM_SCALE * math.log2(math.e)

LANES = 128
HP = HEADS * LANES
ROPE_LO = NOPE

C_CQ = 0
C_CKV = C_CQ + Q_LORA
C_KR = C_CKV + KV_LORA
C_ZA = C_KR + LANES
C_U = C_ZA + A_WIDTH
C_V = C_U + G_WIDTH
C_ZB = C_V + G_WIDTH
C_END = C_ZB + G_WIDTH

TM_PROJ = 512
PROJ_CHUNK = 512
TM_OUT = 2048
OUT_CHUNK = 256
TQ = 1024
TK = 1024
QW = 256
HEADS_PER_STEP = 2
NEG_BIG = -1e30

VMEM_LIMIT = 48 * 1024 * 1024
VMEM_LIMIT_ATTN = 52 * 1024 * 1024

_f32 = jnp.float32
_bf16 = jnp.bfloat16


def _dot(a, b):
    return jnp.dot(a, b, preferred_element_type=_f32)


def _dot_nt(a, b):
    return lax.dot_general(a, b, (((1,), (1,)), ((), ())), preferred_element_type=_f32)


def _gelu(x):
    return 0.5 * x * (1.0 + lax.erf(x * (1.0 / math.sqrt(2.0))))


def _silu(x):
    return x * (1.0 / (1.0 + jnp.exp(-x)))


def _proj_kernel(x_ref, pos_ref, aux_ref, win_ref, gq_ref, wuq_ref, gkv_ref, wuk_ref,
                 wuvt_ref, sg_ref, sb_ref, wsp_ref, bsp_ref,
                 q_out, k_out, vt_out, ga_out, ob_out, cos_sc, sup_sc, sdn_sc):
    tm = x_ref.shape[1]

    ang = pos_ref[0].astype(_f32) * aux_ref[0:1, :]
    cos8 = jnp.cos(ang)
    sin8 = jnp.sin(ang)
    lane = lax.broadcasted_iota(jnp.int32, cos8.shape, 1)
    in_lo = (lane >= ROPE_LO) & (lane < ROPE_LO + HALF)
    in_hi = (lane >= ROPE_LO + HALF) & (lane < ROPE_LO + ROPE)

    def place(t8, lane0, i):
        shift = (lane0 - HALF * i) % LANES
        return t8 if shift == 0 else pltpu.roll(t8, shift, 1)

    for i in range(POS_PER_ROW):
        rows = pl.ds(i, tm // POS_PER_ROW, stride=POS_PER_ROW)
        cos_sc[rows, :] = jnp.where(in_lo, place(cos8, ROPE_LO, i),
                                    jnp.where(in_hi, place(cos8, ROPE_LO + HALF, i), 1.0))
        sup_sc[rows, :] = jnp.where(in_hi, place(sin8, ROPE_LO + HALF, i), 0.0)
        sdn_sc[rows, :] = jnp.where(in_lo, -place(sin8, ROPE_LO, i), 0.0)

    def rms(c, g_ref):
        return c * lax.rsqrt(jnp.mean(c * c, axis=-1, keepdims=True) + EPS) * g_ref[...]

    row = lax.broadcasted_iota(jnp.int32, (CHUNK, 2 * CHUNK), 0)
    col = lax.broadcasted_iota(jnp.int32, (CHUNK, 2 * CHUNK), 1)
    tri = (col % CHUNK) <= row
    lo_half = lax.broadcasted_iota(jnp.int32, (CHUNK, LANES), 1) < G_HEAD_DIM
    vt_row = lax.broadcasted_iota(jnp.int32, (LANES, PROJ_CHUNK), 0)
    own_rows = [vt_row < V_DIM, vt_row >= V_DIM]
    nope_lanes = lax.broadcasted_iota(jnp.int32, (PROJ_CHUNK, LANES), 1) < NOPE
    fill_rows = [jnp.where(vt_row == ONES_ROW[par], 1.0, 0.0) for par in range(2)]

    for rc in range(tm // PROJ_CHUNK):
        rs = slice(rc * PROJ_CHUNK, (rc + 1) * PROJ_CHUNK)
        x = x_ref[0, rs, :].astype(_bf16)

        def proj(lo, hi, x=x):
            return _dot(x, win_ref[:, lo:hi])

        def rope(t, rs=rs):
            return (t * cos_sc[rs, :] + pltpu.roll(t, HALF, 1) * sup_sc[rs, :]
                    + pltpu.roll(t, LANES - HALF, 1) * sdn_sc[rs, :])

        lat = proj(C_CQ, C_ZA)
        v_pre = proj(C_V, C_ZB)
        u_pre = proj(C_U, C_V)
        cqn = rms(lat[:, C_CQ:C_CKV], gq_ref).astype(_bf16)
        q = _dot(cqn, wuq_ref[...]) * Q_SCALE
        ckn = rms(lat[:, C_CKV:C_KR], gkv_ref).astype(_bf16)
        kn = _dot(ckn, wuk_ref[...])
        vt = _dot_nt(wuvt_ref[...], ckn)
        kr = rope(lat[:, C_KR:C_ZA])
        for h in range(HEADS):
            sl = slice(h * LANES, (h + 1) * LANES)
            q_out[0, h, rs, :] = rope(q[:, sl]).astype(_bf16)
            kn_pair = kn[:, (h // 2) * LANES:(h // 2 + 1) * LANES]
            kn_h = kn_pair if h % 2 == 0 else pltpu.roll(kn_pair, NOPE, 1)
            k_out[0, h, rs, :] = (jnp.where(nope_lanes, kn_h, 0.0) + kr).astype(_bf16)
            vt_pair = vt[(h // 2) * LANES:(h // 2 + 1) * LANES, :]
            vt_out[0, h, :, rs] = jnp.where(own_rows[h % 2], vt_pair, fill_rows[h % 2]).astype(_bf16)

        zb_pre = proj(C_ZB, C_END)
        ga_out[0, rs, :] = _silu(proj(C_ZA, C_U)).astype(_bf16)

        u = _gelu(u_pre)
        v = _gelu(v_pre)
        mu = jnp.mean(v, axis=-1, keepdims=True)
        vc = v - mu
        var = jnp.mean(vc * vc, axis=-1, keepdims=True)
        vn = (vc * lax.rsqrt(var + EPS) * sg_ref[...] + sb_ref[...])
        gb = _silu(zb_pre)

        for pair in range(G_HEADS // 2):
            w_pair = jnp.where(tri, wsp_ref[pair], 0.0).astype(_bf16)
            ls = slice(pair * LANES, (pair + 1) * LANES)
            n_chunks = PROJ_CHUNK // CHUNK
            stacked = []
            for c in range(n_chunks):
                vp = vn[c * CHUNK:(c + 1) * CHUNK, ls]
                stacked.append(jnp.concatenate(
                    [jnp.where(lo_half, vp, 0.0), jnp.where(lo_half, 0.0, vp)], axis=0
                ).astype(_bf16))
            sv_all = _dot(w_pair, jnp.concatenate(stacked, axis=1))
            for c in range(n_chunks):
                cs = slice(c * CHUNK, (c + 1) * CHUNK)
                os = slice(rc * PROJ_CHUNK + c * CHUNK, rc * PROJ_CHUNK + (c + 1) * CHUNK)
                sv = sv_all[:, c * LANES:(c + 1) * LANES] + bsp_ref[:, ls]
                ob_out[0, os, ls] = (u[cs, ls] * sv * gb[cs, ls]).astype(_bf16)


def _attn_kernel(q_ref, k_ref, vt_ref, g_ref, o_ref, s_sc, cmax_sc, m_sc, acc_sc):
    seq = q_ref.shape[2]
    nq = seq // TQ

    def causal(s, t):
        r = lax.broadcasted_iota(jnp.int32, s.shape, 0)
        c = lax.broadcasted_iota(jnp.int32, s.shape, 1)
        return jnp.where(r - c <= t, s, NEG_BIG)

    def scores(hh, slot, q0, j, c0=0, c1=TQ, r0=0, r1=TK, koff=None):
        ks = pl.multiple_of(j * TK + r0, QW)
        k = k_ref[0, hh, pl.ds(ks, r1 - r0), :]
        q = q_ref[0, hh, pl.ds(pl.multiple_of(q0 + c0, QW), c1 - c0), :]
        s = _dot_nt(k, q)
        if koff is not None:
            s = causal(s, c0 - r0 - koff)
        s_sc[hh, slot, r0:r1, c0:c1] = s
        cmax_sc[hh, slot, :, c0:c1] = jnp.max(s, axis=0, keepdims=True)

    def softmax_pv(hh, slot, j, c0=0, c1=TQ, r0=0, r1=TK, koff_here=None):
        ks = pl.multiple_of(j * TK + r0, QW)
        s = s_sc[hh, slot, r0:r1, c0:c1]
        if koff_here is not None:
            s = causal(s, c0 - r0 - koff_here)
            m_cur = jnp.max(s, axis=0, keepdims=True)
        else:
            m_cur = cmax_sc[hh, slot, :, c0:c1]
        m_prev = m_sc[hh, :, c0:c1]
        m_next = jnp.maximum(m_prev, m_cur)
        p = jnp.exp2(s - m_next).astype(_bf16)
        alpha = jnp.exp2(m_prev - m_next)
        vt = vt_ref[0, hh, :, pl.ds(ks, r1 - r0)]
        acc_sc[hh, :, c0:c1] = alpha * acc_sc[hh, :, c0:c1] + _dot(vt, p)
        m_sc[hh, :, c0:c1] = m_next

    def diag_quarters(koff):
        out = []
        for c0 in range(0, TQ, QW):
            last_key = c0 + QW - 1 - koff
            if last_key < 0:
                continue
            r1 = min(TK, (last_key // QW + 1) * QW)
            out.append((c0, c0 + QW, r1, c0 - koff < r1 - 1))
        return out

    scores(0, 0, 0, 0)

    def q_tile(qi, carry):
        q0 = pl.multiple_of(qi * TQ, TQ)
        m_sc[...] = jnp.full(m_sc.shape, NEG_BIG, _f32)
        acc_sc[...] = jnp.zeros(acc_sc.shape, _f32)

        def step(nxt, cur):
            for c0 in range(0, TQ, QW):
                scores(nxt[0], nxt[1], q0, nxt[2], c0, c0 + QW)
                softmax_pv(cur[0], cur[1], cur[2], c0, c0 + QW)

        def one_block(j):
            step((1, 0, j), (0, 0, j))
            step((0, 0, j + 1), (1, 0, j))

        def two_blocks(i, c):
            one_block(2 * i)
            one_block(2 * i + 1)
            return c

        def last_block(i, c):
            one_block(2 * (qi // 2))
            return c

        lax.fori_loop(0, qi // 2, two_blocks, 0)
        lax.fori_loop(0, qi % 2, last_block, 0)
        j0 = qi
        q_next = jnp.minimum(qi + 1, nq - 1) * TQ
        full = [(c0, c0 + QW, TK, False) for c0 in range(0, TQ, QW)]
        units = [dict(hh=0, slot=0, q0=q0, j=j0, koff=0, parts=diag_quarters(0), stored_masked=False),
                 dict(hh=1, slot=0, q0=q0, j=j0, koff=0, parts=diag_quarters(0), stored_masked=True),
                 dict(hh=0, slot=0, q0=q_next, j=0, koff=None, parts=full, stored_masked=True)]
        for cur, nxt in zip(units[:-1], units[1:]):
            for i in range(max(len(cur["parts"]), len(nxt["parts"]))):
                if i < len(nxt["parts"]):
                    c0, c1, r1, masked = nxt["parts"][i]
                    scores(nxt["hh"], nxt["slot"], nxt["q0"], nxt["j"], c0, c1, 0, r1,
                           nxt["koff"] if masked else None)
                if i < len(cur["parts"]):
                    c0, c1, r1, masked = cur["parts"][i]
                    mask_now = masked and not cur["stored_masked"]
                    softmax_pv(cur["hh"], cur["slot"], cur["j"], c0, c1, 0, r1,
                               cur["koff"] if mask_now else None)

        acc0, acc1 = acc_sc[0], acc_sc[1]
        o0 = acc0 * (1.0 / acc0[ONES_ROW[0]:ONES_ROW[0] + 1, :])
        o1 = acc1 * (1.0 / acc1[ONES_ROW[1]:ONES_ROW[1] + 1, :])
        row = lax.broadcasted_iota(jnp.int32, (LANES, TQ), 0)
        both = jnp.where(row < V_DIM, o0, o1).T
        rows = pl.ds(q0, TQ)
        o_ref[0, rows, :] = (both * g_ref[0, rows, :].astype(_f32)).astype(_bf16)
        return carry

    lax.fori_loop(0, nq, q_tile, 0)


def _out_kernel(x_ref, oa_ref, ob_ref, wo_ref, g_ref, b_ref, o_ref):
    for c in range(TM_OUT // OUT_CHUNK):
        rs = slice(c * OUT_CHUNK, (c + 1) * OUT_CHUNK)
        y = _dot(oa_ref[rs, :], wo_ref[:A_WIDTH, :]) + _dot(ob_ref[rs, :], wo_ref[A_WIDTH:, :])
        z = DN_ALPHA * x_ref[rs, :] + y
        mu = jnp.mean(z, axis=-1, keepdims=True)
        zc = z - mu
        var = jnp.mean(zc * zc, axis=-1, keepdims=True)
        o_ref[rs, :] = zc * lax.rsqrt(var + EPS) * g_ref[...] + b_ref[...]


def _prep_weights(w_in, w_uq, w_ukv, w_spatial, b_spatial):
    kr0 = Q_LORA + KV_LORA
    zeros = lambda n: jnp.zeros((D_MODEL, n), w_in.dtype)
    win_p = jnp.concatenate([w_in[:, :kr0], zeros(ROPE_LO), w_in[:, kr0:kr0 + ROPE],
                             zeros(LANES - ROPE_LO - ROPE), w_in[:, kr0 + ROPE:]],
                            axis=1).astype(_bf16)

    wq = w_uq.reshape(Q_LORA, HEADS, NOPE + ROPE)
    wq = jnp.pad(wq, ((0, 0), (0, 0), (0, LANES - NOPE - ROPE))).reshape(Q_LORA, HP)
    wkv = w_ukv.reshape(KV_LORA, HEADS, NOPE + V_DIM)
    wk = wkv[:, :, :NOPE].reshape(KV_LORA, HEADS * NOPE)
    wvt = wkv[:, :, NOPE:].reshape(KV_LORA, A_WIDTH).T

    wsp = w_spatial.reshape(G_HEADS // 2, 2, CHUNK, CHUNK).transpose(0, 2, 1, 3)
    wsp = wsp.reshape(G_HEADS // 2, CHUNK, 2 * CHUNK)
    bsp = jnp.repeat(b_spatial.T, G_HEAD_DIM, axis=1)
    return win_p, wq.astype(_bf16), wk.astype(_bf16), wvt.astype(_bf16), wsp, bsp


def _aux_table():
    inv_freq = 1.0 / (ROPE_THETA ** (jnp.arange(HALF, dtype=_f32) / HALF))
    freq = jnp.tile(inv_freq, POS_PER_ROW)
    return jnp.concatenate([freq[None], jnp.zeros((7, LANES), _f32)], axis=0)


def _pack_positions(positions):
    b, s = positions.shape
    return jnp.repeat(positions.reshape(b, s // POS_PER_ROW, POS_PER_ROW), HALF, axis=-1)


def kernel(x, positions, w_in, q_norm_g, w_uq, kv_norm_g, w_ukv, sgu_norm_g, sgu_norm_b,
           w_spatial, b_spatial, w_out, ln_g, ln_b):
    b, s, d = x.shape
    assert d == D_MODEL and s % TQ == 0 and s % TM_PROJ == 0 and TQ == TK and HEADS_PER_STEP == 2
    win_p, wq, wk, wvt, wsp, bsp = _prep_weights(w_in, w_uq, w_ukv, w_spatial, b_spatial)
    aux = _aux_table()
    pos8 = _pack_positions(positions)

    const2 = lambda shape: pl.BlockSpec(shape, lambda bi, i: (0, 0))
    const3 = lambda shape: pl.BlockSpec(shape, lambda bi, i: (0, 0, 0))
    head_spec = pl.BlockSpec((1, HEADS, TM_PROJ, LANES), lambda bi, i: (bi, 0, i, 0))
    row_spec = lambda w: pl.BlockSpec((1, TM_PROJ, w), lambda bi, i: (bi, i, 0))
    hshape = jax.ShapeDtypeStruct((b, HEADS, s, LANES), _bf16)
    qp, kp, vtp, ga, ob = pl.pallas_call(
        _proj_kernel,
        grid=(b, s // TM_PROJ),
        in_specs=[row_spec(D_MODEL),
                  pl.BlockSpec((1, TM_PROJ // POS_PER_ROW, LANES), lambda bi, i: (bi, i, 0)),
                  const2((8, LANES)),
                  const2((D_MODEL, C_END)), const2((1, Q_LORA)), const2((Q_LORA, HP)),
                  const2((1, KV_LORA)), const2((KV_LORA, HEADS * NOPE)), const2((A_WIDTH, KV_LORA)),
                  const2((1, G_WIDTH)), const2((1, G_WIDTH)),
                  const3((G_HEADS // 2, CHUNK, 2 * CHUNK)), const2((CHUNK, G_WIDTH))],
        out_specs=[head_spec, head_spec,
                   pl.BlockSpec((1, HEADS, LANES, TM_PROJ), lambda bi, i: (bi, 0, 0, i)),
                   row_spec(A_WIDTH), row_spec(G_WIDTH)],
        out_shape=[hshape, hshape, jax.ShapeDtypeStruct((b, HEADS, LANES, s), _bf16),
                   jax.ShapeDtypeStruct((b, s, A_WIDTH), _bf16),
                   jax.ShapeDtypeStruct((b, s, G_WIDTH), _bf16)],
        scratch_shapes=[pltpu.VMEM((TM_PROJ, LANES), _f32)] * 3,
        compiler_params=pltpu.CompilerParams(
            dimension_semantics=("arbitrary", "arbitrary"), vmem_limit_bytes=VMEM_LIMIT),
        name="hyb_proj",
    )(x, pos8, aux, win_p, q_norm_g.reshape(1, -1), wq, kv_norm_g.reshape(1, -1), wk, wvt,
      sgu_norm_g.reshape(1, -1), sgu_norm_b.reshape(1, -1), wsp, bsp)

    hps = HEADS_PER_STEP
    per_pair = pl.BlockSpec((1, hps, s, LANES), lambda bi, hp: (bi, hp, 0, 0))
    gate_spec = pl.BlockSpec((1, s, LANES), lambda bi, hp: (bi, 0, hp))
    oa = pl.pallas_call(
        _attn_kernel,
        grid=(b, HEADS // hps),
        in_specs=[per_pair, per_pair,
                  pl.BlockSpec((1, hps, LANES, s), lambda bi, hp: (bi, hp, 0, 0)), gate_spec],
        out_specs=gate_spec,
        out_shape=jax.ShapeDtypeStruct((b, s, A_WIDTH), _bf16),
        scratch_shapes=[pltpu.VMEM((hps, 1, TK, TQ), _f32), pltpu.VMEM((hps, 1, 1, TQ), _f32),
                        pltpu.VMEM((hps, 1, TQ), _f32), pltpu.VMEM((hps, LANES, TQ), _f32)],
        compiler_params=pltpu.CompilerParams(
            dimension_semantics=("arbitrary", "arbitrary"), vmem_limit_bytes=VMEM_LIMIT_ATTN),
        name="hyb_attn",
    )(qp, kp, vtp, ga)

    r = b * s
    rows = lambda w: pl.BlockSpec((TM_OUT, w), lambda i: (i, 0))
    fixed = lambda shape: pl.BlockSpec(shape, lambda i: (0, 0))
    out = pl.pallas_call(
        _out_kernel,
        grid=(r // TM_OUT,),
        in_specs=[rows(D_MODEL), rows(A_WIDTH), rows(G_WIDTH), fixed((D_MODEL, D_MODEL)),
                  fixed((1, D_MODEL)), fixed((1, D_MODEL))],
        out_specs=rows(D_MODEL),
        out_shape=jax.ShapeDtypeStruct((r, D_MODEL), x.dtype),
        compiler_params=pltpu.CompilerParams(
            dimension_semantics=("arbitrary",), vmem_limit_bytes=VMEM_LIMIT),
        name="hyb_out",
    )(x.reshape(r, d), oa.reshape(r, A_WIDTH), ob.reshape(r, G_WIDTH), w_out.astype(_bf16),
      ln_g.reshape(1, -1), ln_b.reshape(1, -1))
    return out.reshape(b, s, d)
```

```python
import math

import jax
import jax.numpy as jnp
from jax import lax
from jax.experimental import pallas as pl
from jax.experimental.pallas import tpu as pltpu

D_MODEL = 1024
HEADS = 8
NOPE = 64
ROPE = 32
HALF = ROPE // 2
POS_PER_ROW = 8
V_DIM = 64
ONES_ROW = (V_DIM, 0)
A_WIDTH = HEADS * V_DIM
Q_LORA = 256
KV_LORA = 128
ROPE_THETA = 10000.0
CHUNK = 128
G_HEADS = 8
G_WIDTH = 512
G_HEAD_DIM = G_WIDTH // G_HEADS
DN_ALPHA = 2.0 ** 0.25
EPS = 1e-5
SM_SCALE = 1.0 / math.sqrt(NOPE + ROPE)
Q_SCALE = SM_SCALE * math.log2(math.e)

LANES = 128
HP = HEADS * LANES
ROPE_LO = NOPE

C_CQ = 0
C_CKV = C_CQ + Q_LORA
C_KR = C_CKV + KV_LORA
C_ZA = C_KR + LANES
C_U = C_ZA + A_WIDTH
C_V = C_U + G_WIDTH
C_ZB = C_V + G_WIDTH
C_END = C_ZB + G_WIDTH

TM_PROJ = 512
PROJ_CHUNK = 512
TM_OUT = 2048
OUT_CHUNK = 256
TQ = 1024
TK = 1024
QW = 256
HEADS_PER_STEP = 2
NEG_BIG = -1e30

VMEM_LIMIT = 48 * 1024 * 1024
VMEM_LIMIT_ATTN = 52 * 1024 * 1024

_f32 = jnp.float32
_bf16 = jnp.bfloat16


def _dot(a, b):
    return jnp.dot(a, b, preferred_element_type=_f32)


def _dot_nt(a, b):
    return lax.dot_general(a, b, (((1,), (1,)), ((), ())), preferred_element_type=_f32)


def _gelu(x):
    return 0.5 * x * (1.0 + lax.erf(x * (1.0 / math.sqrt(2.0))))


def _silu(x):
    return x * (1.0 / (1.0 + jnp.exp(-x)))


def _proj_kernel(x_ref, pos_ref, aux_ref, win_ref, gq_ref, wuq_ref, gkv_ref, wuk_ref,
                 wuvt_ref, sg_ref, sb_ref, wsp_ref, bsp_ref,
                 q_out, k_out, vt_out, ga_out, ob_out, cos_sc, sup_sc, sdn_sc):
    tm = x_ref.shape[1]

    ang = pos_ref[0].astype(_f32) * aux_ref[0:1, :]
    cos8 = jnp.cos(ang)
    sin8 = jnp.sin(ang)
    lane = lax.broadcasted_iota(jnp.int32, cos8.shape, 1)
    in_lo = (lane >= ROPE_LO) & (lane < ROPE_LO + HALF)
    in_hi = (lane >= ROPE_LO + HALF) & (lane < ROPE_LO + ROPE)

    def place(t8, lane0, i):
        shift = (lane0 - HALF * i) % LANES
        return t8 if shift == 0 else pltpu.roll(t8, shift, 1)

    for i in range(POS_PER_ROW):
        rows = pl.ds(i, tm // POS_PER_ROW, stride=POS_PER_ROW)
        cos_sc[rows, :] = jnp.where(in_lo, place(cos8, ROPE_LO, i),
                                    jnp.where(in_hi, place(cos8, ROPE_LO + HALF, i), 1.0))
        sup_sc[rows, :] = jnp.where(in_hi, place(sin8, ROPE_LO + HALF, i), 0.0)
        sdn_sc[rows, :] = jnp.where(in_lo, -place(sin8, ROPE_LO, i), 0.0)

    def rms(c, g_ref):
        return c * lax.rsqrt(jnp.mean(c * c, axis=-1, keepdims=True) + EPS) * g_ref[...]

    row = lax.broadcasted_iota(jnp.int32, (CHUNK, 2 * CHUNK), 0)
    col = lax.broadcasted_iota(jnp.int32, (CHUNK, 2 * CHUNK), 1)
    tri = (col % CHUNK) <= row
    lo_half = lax.broadcasted_iota(jnp.int32, (CHUNK, LANES), 1) < G_HEAD_DIM
    vt_row = lax.broadcasted_iota(jnp.int32, (LANES, PROJ_CHUNK), 0)
    own_rows = [vt_row < V_DIM, vt_row >= V_DIM]
    nope_lanes = lax.broadcasted_iota(jnp.int32, (PROJ_CHUNK, LANES), 1) < NOPE
    fill_rows = [jnp.where(vt_row == ONES_ROW[par], 1.0, 0.0) for par in range(2)]

    for rc in range(tm // PROJ_CHUNK):
        rs = slice(rc * PROJ_CHUNK, (rc + 1) * PROJ_CHUNK)
        x = x_ref[0, rs, :].astype(_bf16)

        def proj(lo, hi, x=x):
            return _dot(x, win_ref[:, lo:hi])

        def rope(t, rs=rs):
            return (t * cos_sc[rs, :] + pltpu.roll(t, HALF, 1) * sup_sc[rs, :]
                    + pltpu.roll(t, LANES - HALF, 1) * sdn_sc[rs, :])

        lat = proj(C_CQ, C_ZA)
        v_pre = proj(C_V, C_ZB)
        u_pre = proj(C_U, C_V)
        cqn = rms(lat[:, C_CQ:C_CKV], gq_ref).astype(_bf16)
        q = _dot(cqn, wuq_ref[...]) * Q_SCALE
        ckn = rms(lat[:, C_CKV:C_KR], gkv_ref).astype(_bf16)
        kn = _dot(ckn, wuk_ref[...])
        vt = _dot_nt(wuvt_ref[...], ckn)
        kr = rope(lat[:, C_KR:C_ZA])
        for h in range(HEADS):
            sl = slice(h * LANES, (h + 1) * LANES)
            q_out[0, h, rs, :] = rope(q[:, sl]).astype(_bf16)
            kn_pair = kn[:, (h // 2) * LANES:(h // 2 + 1) * LANES]
            kn_h = kn_pair if h % 2 == 0 else pltpu.roll(kn_pair, NOPE, 1)
            k_out[0, h, rs, :] = (jnp.where(nope_lanes, kn_h, 0.0) + kr).astype(_bf16)
            vt_pair = vt[(h // 2) * LANES:(h // 2 + 1) * LANES, :]
            vt_out[0, h, :, rs] = jnp.where(own_rows[h % 2], vt_pair, fill_rows[h % 2]).astype(_bf16)

        zb_pre = proj(C_ZB, C_END)
        ga_out[0, rs, :] = _silu(proj(C_ZA, C_U)).astype(_bf16)

        u = _gelu(u_pre)
        v = _gelu(v_pre)
        mu = jnp.mean(v, axis=-1, keepdims=True)
        vc = v - mu
        var = jnp.mean(vc * vc, axis=-1, keepdims=True)
        vn = (vc * lax.rsqrt(var + EPS) * sg_ref[...] + sb_ref[...])
        gb = _silu(zb_pre)

        for pair in range(G_HEADS // 2):
            w_pair = jnp.where(tri, wsp_ref[pair], 0.0).astype(_bf16)
            ls = slice(pair * LANES, (pair + 1) * LANES)
            n_chunks = PROJ_CHUNK // CHUNK
            stacked = []
            for c in range(n_chunks):
                vp = vn[c * CHUNK:(c + 1) * CHUNK, ls]
                stacked.append(jnp.concatenate(
                    [jnp.where(lo_half, vp, 0.0), jnp.where(lo_half, 0.0, vp)], axis=0
                ).astype(_bf16))
            sv_all = _dot(w_pair, jnp.concatenate(stacked, axis=1))
            for c in range(n_chunks):
                cs = slice(c * CHUNK, (c + 1) * CHUNK)
                os = slice(rc * PROJ_CHUNK + c * CHUNK, rc * PROJ_CHUNK + (c + 1) * CHUNK)
                sv = sv_all[:, c * LANES:(c + 1) * LANES] + bsp_ref[:, ls]
                ob_out[0, os, ls] = (u[cs, ls] * sv * gb[cs, ls]).astype(_bf16)


def _attn_kernel(q_ref, k_ref, vt_ref, g_ref, o_ref, s_sc, cmax_sc, m_sc, acc_sc):
    seq = q_ref.shape[2]
    nq = seq // TQ

    def causal(s, t):
        r = lax.broadcasted_iota(jnp.int32, s.shape, 0)
        c = lax.broadcasted_iota(jnp.int32, s.shape, 1)
        return jnp.where(r - c <= t, s, NEG_BIG)

    def scores(hh, q0, j, c0=0, c1=TQ, r0=0, r1=TK, koff=None):
        ks = pl.multiple_of(j * TK + r0, QW)
        k = k_ref[0, hh, pl.ds(ks, r1 - r0), :]
        q = q_ref[0, hh, pl.ds(pl.multiple_of(q0 + c0, QW), c1 - c0), :]
        s = _dot_nt(k, q)
        if koff is not None:
            s = causal(s, c0 - r0 - koff)
        s_sc[hh, r0:r1, c0:c1] = s
        cmax_sc[hh, :, c0:c1] = jnp.max(s, axis=0, keepdims=True)

    def softmax_pv(hh, j, c0=0, c1=TQ, r0=0, r1=TK, koff_here=None):
        ks = pl.multiple_of(j * TK + r0, QW)
        s = s_sc[hh, r0:r1, c0:c1]
        if koff_here is not None:
            s = causal(s, c0 - r0 - koff_here)
            m_cur = jnp.max(s, axis=0, keepdims=True)
        else:
            m_cur = cmax_sc[hh, :, c0:c1]
        m_prev = m_sc[hh, :, c0:c1]
        m_next = jnp.maximum(m_prev, m_cur)
        p = jnp.exp2(s - m_next).astype(_bf16)
        alpha = jnp.exp2(m_prev - m_next)
        vt = vt_ref[0, hh, :, pl.ds(ks, r1 - r0)]
        acc_sc[hh, :, c0:c1] = alpha * acc_sc[hh, :, c0:c1] + _dot(vt, p)
        m_sc[hh, :, c0:c1] = m_next

    def diag_quarters(koff):
        out = []
        for c0 in range(0, TQ, QW):
            last_key = c0 + QW - 1 - koff
            if last_key < 0:
                continue
            r1 = min(TK, (last_key // QW + 1) * QW)
            out.append((c0, c0 + QW, r1, c0 - koff < r1 - 1))
        return out

    scores(0, 0, 0)

    def q_tile(qi, carry):
        q0 = pl.multiple_of(qi * TQ, TQ)
        m_sc[...] = jnp.full(m_sc.shape, NEG_BIG, _f32)
        acc_sc[...] = jnp.zeros(acc_sc.shape, _f32)

        def step(nxt, cur):
            for c0 in range(0, TQ, QW):
                scores(nxt[0], q0, nxt[1], c0, c0 + QW)
                softmax_pv(cur[0], cur[1], c0, c0 + QW)

        def one_block(j):
            step((1, j), (0, j))
            step((0, j + 1), (1, j))

        def two_blocks(i, c):
            one_block(2 * i)
            one_block(2 * i + 1)
            return c

        def last_block(i, c):
            one_block(2 * (qi // 2))
            return c

        lax.fori_loop(0, qi // 2, two_blocks, 0)
        lax.fori_loop(0, qi % 2, last_block, 0)
        j0 = qi
        q_next = jnp.minimum(qi + 1, nq - 1) * TQ
        full = [(c0, c0 + QW, TK, False) for c0 in range(0, TQ, QW)]
        units = [dict(hh=0, q0=q0, j=j0, koff=0, parts=diag_quarters(0), stored_masked=False),
                 dict(hh=1, q0=q0, j=j0, koff=0, parts=diag_quarters(0), stored_masked=True),
                 dict(hh=0, q0=q_next, j=0, koff=None, parts=full, stored_masked=True)]
        for cur, nxt in zip(units[:-1], units[1:]):
            for i in range(max(len(cur["parts"]), len(nxt["parts"]))):
                if i < len(nxt["parts"]):
                    c0, c1, r1, masked = nxt["parts"][i]
                    scores(nxt["hh"], nxt["q0"], nxt["j"], c0, c1, 0, r1,
                           nxt["koff"] if masked else None)
                if i < len(cur["parts"]):
                    c0, c1, r1, masked = cur["parts"][i]
                    mask_now = masked and not cur["stored_masked"]
                    softmax_pv(cur["hh"], cur["j"], c0, c1, 0, r1,
                               cur["koff"] if mask_now else None)

        acc0, acc1 = acc_sc[0], acc_sc[1]
        o0 = acc0 * (1.0 / acc0[ONES_ROW[0]:ONES_ROW[0] + 1, :])
        o1 = acc1 * (1.0 / acc1[ONES_ROW[1]:ONES_ROW[1] + 1, :])
        row = lax.broadcasted_iota(jnp.int32, (LANES, TQ), 0)
        both = jnp.where(row < V_DIM, o0, o1).T
        rows = pl.ds(q0, TQ)
        o_ref[0, rows, :] = (both * g_ref[0, rows, :].astype(_f32)).astype(_bf16)
        return carry

    lax.fori_loop(0, nq, q_tile, 0)


def _out_kernel(x_ref, oa_ref, ob_ref, wo_ref, g_ref, b_ref, o_ref):
    for c in range(TM_OUT // OUT_CHUNK):
        rs = slice(c * OUT_CHUNK, (c + 1) * OUT_CHUNK)
        y = _dot(oa_ref[rs, :], wo_ref[:A_WIDTH, :]) + _dot(ob_ref[rs, :], wo_ref[A_WIDTH:, :])
        z = DN_ALPHA * x_ref[rs, :] + y
        mu = jnp.mean(z, axis=-1, keepdims=True)
        zc = z - mu
        var = jnp.mean(zc * zc, axis=-1, keepdims=True)
        o_ref[rs, :] = zc * lax.rsqrt(var + EPS) * g_ref[...] + b_ref[...]


def _prep_weights(w_in, w_uq, w_ukv, w_spatial, b_spatial):
    kr0 = Q_LORA + KV_LORA
    zeros = lambda n: jnp.zeros((D_MODEL, n), w_in.dtype)
    win_p = jnp.concatenate([w_in[:, :kr0], zeros(ROPE_LO), w_in[:, kr0:kr0 + ROPE],
                             zeros(LANES - ROPE_LO - ROPE), w_in[:, kr0 + ROPE:]],
                            axis=1).astype(_bf16)

    wq = w_uq.reshape(Q_LORA, HEADS, NOPE + ROPE)
    wq = jnp.pad(wq, ((0, 0), (0, 0), (0, LANES - NOPE - ROPE))).reshape(Q_LORA, HP)
    wkv = w_ukv.reshape(KV_LORA, HEADS, NOPE + V_DIM)
    wk = wkv[:, :, :NOPE].reshape(KV_LORA, HEADS * NOPE)
    wvt = wkv[:, :, NOPE:].reshape(KV_LORA, A_WIDTH).T

    wsp = w_spatial.reshape(G_HEADS // 2, 2, CHUNK, CHUNK).transpose(0, 2, 1, 3)
    wsp = wsp.reshape(G_HEADS // 2, CHUNK, 2 * CHUNK)
    bsp = jnp.repeat(b_spatial.T, G_HEAD_DIM, axis=1)
    return win_p, wq.astype(_bf16), wk.astype(_bf16), wvt.astype(_bf16), wsp, bsp


def _aux_table():
    inv_freq = 1.0 / (ROPE_THETA ** (jnp.arange(HALF, dtype=_f32) / HALF))
    freq = jnp.tile(inv_freq, POS_PER_ROW)
    return jnp.concatenate([freq[None], jnp.zeros((7, LANES), _f32)], axis=0)


def _pack_positions(positions):
    b, s = positions.shape
    return jnp.repeat(positions.reshape(b, s // POS_PER_ROW, POS_PER_ROW), HALF, axis=-1)


def kernel(x, positions, w_in, q_norm_g, w_uq, kv_norm_g, w_ukv, sgu_norm_g, sgu_norm_b,
           w_spatial, b_spatial, w_out, ln_g, ln_b):
    b, s, d = x.shape
    assert d == D_MODEL and s % TQ == 0 and s % TM_PROJ == 0 and TQ == TK and HEADS_PER_STEP == 2
    win_p, wq, wk, wvt, wsp, bsp = _prep_weights(w_in, w_uq, w_ukv, w_spatial, b_spatial)
    aux = _aux_table()
    pos8 = _pack_positions(positions)

    const2 = lambda shape: pl.BlockSpec(shape, lambda bi, i: (0, 0))
    const3 = lambda shape: pl.BlockSpec(shape, lambda bi, i: (0, 0, 0))
    head_spec = pl.BlockSpec((1, HEADS, TM_PROJ, LANES), lambda bi, i: (bi, 0, i, 0))
    row_spec = lambda w: pl.BlockSpec((1, TM_PROJ, w), lambda bi, i: (bi, i, 0))
    hshape = jax.ShapeDtypeStruct((b, HEADS, s, LANES), _bf16)
    qp, kp, vtp, ga, ob = pl.pallas_call(
        _proj_kernel,
        grid=(b, s // TM_PROJ),
        in_specs=[row_spec(D_MODEL),
                  pl.BlockSpec((1, TM_PROJ // POS_PER_ROW, LANES), lambda bi, i: (bi, i, 0)),
                  const2((8, LANES)),
                  const2((D_MODEL, C_END)), const2((1, Q_LORA)), const2((Q_LORA, HP)),
                  const2((1, KV_LORA)), const2((KV_LORA, HEADS * NOPE)), const2((A_WIDTH, KV_LORA)),
                  const2((1, G_WIDTH)), const2((1, G_WIDTH)),
                  const3((G_HEADS // 2, CHUNK, 2 * CHUNK)), const2((CHUNK, G_WIDTH))],
        out_specs=[head_spec, head_spec,
                   pl.BlockSpec((1, HEADS, LANES, TM_PROJ), lambda bi, i: (bi, 0, 0, i)),
                   row_spec(A_WIDTH), row_spec(G_WIDTH)],
        out_shape=[hshape, hshape, jax.ShapeDtypeStruct((b, HEADS, LANES, s), _bf16),
                   jax.ShapeDtypeStruct((b, s, A_WIDTH), _bf16),
                   jax.ShapeDtypeStruct((b, s, G_WIDTH), _bf16)],
        scratch_shapes=[pltpu.VMEM((TM_PROJ, LANES), _f32)] * 3,
        compiler_params=pltpu.CompilerParams(
            dimension_semantics=("arbitrary", "arbitrary"), vmem_limit_bytes=VMEM_LIMIT),
        name="hyb_proj",
    )(x, pos8, aux, win_p, q_norm_g.reshape(1, -1), wq, kv_norm_g.reshape(1, -1), wk, wvt,
      sgu_norm_g.reshape(1, -1), sgu_norm_b.reshape(1, -1), wsp, bsp)

    hps = HEADS_PER_STEP
    per_pair = pl.BlockSpec((1, hps, s, LANES), lambda bi, hp: (bi, hp, 0, 0))
    gate_spec = pl.BlockSpec((1, s, LANES), lambda bi, hp: (bi, 0, hp))
    oa = pl.pallas_call(
        _attn_kernel,
        grid=(b, HEADS // hps),
        in_specs=[per_pair, per_pair,
                  pl.BlockSpec((1, hps, LANES, s), lambda bi, hp: (bi, hp, 0, 0)), gate_spec],
        out_specs=gate_spec,
        out_shape=jax.ShapeDtypeStruct((b, s, A_WIDTH), _bf16),
        scratch_shapes=[pltpu.VMEM((hps, TK, TQ), _f32), pltpu.VMEM((hps, 1, TQ), _f32),
                        pltpu.VMEM((hps, 1, TQ), _f32), pltpu.VMEM((hps, LANES, TQ), _f32)],
        compiler_params=pltpu.CompilerParams(
            dimension_semantics=("arbitrary", "arbitrary"), vmem_limit_bytes=VMEM_LIMIT_ATTN),
        name="hyb_attn",
    )(qp, kp, vtp, ga)

    r = b * s
    rows = lambda w: pl.BlockSpec((TM_OUT, w), lambda i: (i, 0))
    fixed = lambda shape: pl.BlockSpec(shape, lambda i: (0, 0))
    out = pl.pallas_call(
        _out_kernel,
        grid=(r // TM_OUT,),
        in_specs=[rows(D_MODEL), rows(A_WIDTH), rows(G_WIDTH), fixed((D_MODEL, D_MODEL)),
                  fixed((1, D_MODEL)), fixed((1, D_MODEL))],
        out_specs=rows(D_MODEL),
        out_shape=jax.ShapeDtypeStruct((r, D_MODEL), x.dtype),
        compiler_params=pltpu.CompilerParams(
            dimension_semantics=("arbitrary",), vmem_limit_bytes=VMEM_LIMIT),
        name="hyb_out",
    )(x.reshape(r, d), oa.reshape(r, A_WIDTH), ob.reshape(r, G_WIDTH), w_out.astype(_bf16),
      ln_g.reshape(1, -1), ln_b.reshape(1, -1))
    return out.reshape(b, s, d)
```

```python
import math

import jax
import jax.numpy as jnp
from jax import lax
from jax.experimental import pallas as pl
from jax.experimental.pallas import tpu as pltpu

D_MODEL = 1024
HEADS = 8
NOPE = 64
ROPE = 32
HALF = ROPE // 2
POS_PER_ROW = 8
V_DIM = 64
ONES_ROW = (V_DIM, 0)
A_WIDTH = HEADS * V_DIM
Q_LORA = 256
KV_LORA = 128
ROPE_THETA = 10000.0
CHUNK = 128
G_HEADS = 8
G_WIDTH = 512
G_HEAD_DIM = G_WIDTH // G_HEADS
DN_ALPHA = 2.0 ** 0.25
EPS = 1e-5
SM_SCALE = 1.0 / math.sqrt(NOPE + ROPE)
Q_SCALE = SM_SCALE * math.log2(math.e)

LANES = 128
HP = HEADS * LANES
ROPE_LO = NOPE

C_CQ = 0
C_CKV = C_CQ + Q_LORA
C_KR = C_CKV + KV_LORA
C_ZA = C_KR + LANES
C_U = C_ZA + A_WIDTH
C_V = C_U + G_WIDTH
C_ZB = C_V + G_WIDTH
C_END = C_ZB + G_WIDTH

TM_PROJ = 512
PROJ_CHUNK = 512
TM_OUT = 2048
OUT_CHUNK = 256
TQ = 2048
TK = 1024
QW = 256
HEADS_PER_STEP = 2
NEG_BIG = -1e30

VMEM_LIMIT = 48 * 1024 * 1024
VMEM_LIMIT_ATTN = 56 * 1024 * 1024

_f32 = jnp.float32
_bf16 = jnp.bfloat16


def _dot(a, b):
    return jnp.dot(a, b, preferred_element_type=_f32)


def _dot_nt(a, b):
    return lax.dot_general(a, b, (((1,), (1,)), ((), ())), preferred_element_type=_f32)


def _gelu(x):
    return 0.5 * x * (1.0 + lax.erf(x * (1.0 / math.sqrt(2.0))))


def _silu(x):
    return x * (1.0 / (1.0 + jnp.exp(-x)))


def _proj_kernel(x_ref, pos_ref, aux_ref, win_ref, gq_ref, wuq_ref, gkv_ref, wuk_ref,
                 wuvt_ref, sg_ref, sb_ref, wsp_ref, bsp_ref,
                 q_out, k_out, vt_out, ga_out, ob_out, cos_sc, sup_sc, sdn_sc):
    tm = x_ref.shape[1]

    ang = pos_ref[0].astype(_f32) * aux_ref[0:1, :]
    cos8 = jnp.cos(ang)
    sin8 = jnp.sin(ang)
    lane = lax.broadcasted_iota(jnp.int32, cos8.shape, 1)
    in_lo = (lane >= ROPE_LO) & (lane < ROPE_LO + HALF)
    in_hi = (lane >= ROPE_LO + HALF) & (lane < ROPE_LO + ROPE)

    def place(t8, lane0, i):
        shift = (lane0 - HALF * i) % LANES
        return t8 if shift == 0 else pltpu.roll(t8, shift, 1)

    for i in range(POS_PER_ROW):
        rows = pl.ds(i, tm // POS_PER_ROW, stride=POS_PER_ROW)
        cos_sc[rows, :] = jnp.where(in_lo, place(cos8, ROPE_LO, i),
                                    jnp.where(in_hi, place(cos8, ROPE_LO + HALF, i), 1.0))
        sup_sc[rows, :] = jnp.where(in_hi, place(sin8, ROPE_LO + HALF, i), 0.0)
        sdn_sc[rows, :] = jnp.where(in_lo, -place(sin8, ROPE_LO, i), 0.0)

    def rms(c, g_ref):
        return c * lax.rsqrt(jnp.mean(c * c, axis=-1, keepdims=True) + EPS) * g_ref[...]

    row = lax.broadcasted_iota(jnp.int32, (CHUNK, 2 * CHUNK), 0)
    col = lax.broadcasted_iota(jnp.int32, (CHUNK, 2 * CHUNK), 1)
    tri = (col % CHUNK) <= row
    lo_half = lax.broadcasted_iota(jnp.int32, (CHUNK, LANES), 1) < G_HEAD_DIM
    vt_row = lax.broadcasted_iota(jnp.int32, (LANES, PROJ_CHUNK), 0)
    own_rows = [vt_row < V_DIM, vt_row >= V_DIM]
    nope_lanes = lax.broadcasted_iota(jnp.int32, (PROJ_CHUNK, LANES), 1) < NOPE
    fill_rows = [jnp.where(vt_row == ONES_ROW[par], 1.0, 0.0) for par in range(2)]

    for rc in range(tm // PROJ_CHUNK):
        rs = slice(rc * PROJ_CHUNK, (rc + 1) * PROJ_CHUNK)
        x = x_ref[0, rs, :].astype(_bf16)

        def proj(lo, hi, x=x):
            return _dot(x, win_ref[:, lo:hi])

        def rope(t, rs=rs):
            return (t * cos_sc[rs, :] + pltpu.roll(t, HALF, 1) * sup_sc[rs, :]
                    + pltpu.roll(t, LANES - HALF, 1) * sdn_sc[rs, :])

        lat = proj(C_CQ, C_ZA)
        v_pre = proj(C_V, C_ZB)
        u_pre = proj(C_U, C_V)
        cqn = rms(lat[:, C_CQ:C_CKV], gq_ref).astype(_bf16)
        q = _dot(cqn, wuq_ref[...]) * Q_SCALE
        ckn = rms(lat[:, C_CKV:C_KR], gkv_ref).astype(_bf16)
        kn = _dot(ckn, wuk_ref[...])
        vt = _dot_nt(wuvt_ref[...], ckn)
        kr = rope(lat[:, C_KR:C_ZA])
        for h in range(HEADS):
            sl = slice(h * LANES, (h + 1) * LANES)
            q_out[0, h, rs, :] = rope(q[:, sl]).astype(_bf16)
            kn_pair = kn[:, (h // 2) * LANES:(h // 2 + 1) * LANES]
            kn_h = kn_pair if h % 2 == 0 else pltpu.roll(kn_pair, NOPE, 1)
            k_out[0, h, rs, :] = (jnp.where(nope_lanes, kn_h, 0.0) + kr).astype(_bf16)
            vt_pair = vt[(h // 2) * LANES:(h // 2 + 1) * LANES, :]
            vt_out[0, h, :, rs] = jnp.where(own_rows[h % 2], vt_pair, fill_rows[h % 2]).astype(_bf16)

        zb_pre = proj(C_ZB, C_END)
        ga_out[0, rs, :] = _silu(proj(C_ZA, C_U)).astype(_bf16)

        u = _gelu(u_pre)
        v = _gelu(v_pre)
        mu = jnp.mean(v, axis=-1, keepdims=True)
        vc = v - mu
        var = jnp.mean(vc * vc, axis=-1, keepdims=True)
        vn = (vc * lax.rsqrt(var + EPS) * sg_ref[...] + sb_ref[...])
        gb = _silu(zb_pre)

        for pair in range(G_HEADS // 2):
            w_pair = jnp.where(tri, wsp_ref[pair], 0.0).astype(_bf16)
            ls = slice(pair * LANES, (pair + 1) * LANES)
            n_chunks = PROJ_CHUNK // CHUNK
            stacked = []
            for c in range(n_chunks):
                vp = vn[c * CHUNK:(c + 1) * CHUNK, ls]
                stacked.append(jnp.concatenate(
                    [jnp.where(lo_half, vp, 0.0), jnp.where(lo_half, 0.0, vp)], axis=0
                ).astype(_bf16))
            sv_all = _dot(w_pair, jnp.concatenate(stacked, axis=1))
            for c in range(n_chunks):
                cs = slice(c * CHUNK, (c + 1) * CHUNK)
                os = slice(rc * PROJ_CHUNK + c * CHUNK, rc * PROJ_CHUNK + (c + 1) * CHUNK)
                sv = sv_all[:, c * LANES:(c + 1) * LANES] + bsp_ref[:, ls]
                ob_out[0, os, ls] = (u[cs, ls] * sv * gb[cs, ls]).astype(_bf16)


def _attn_kernel(q_ref, k_ref, vt_ref, g_ref, o_ref, s_sc, cmax_sc, m_sc, acc_sc):
    seq = q_ref.shape[2]
    nq = seq // TQ

    def causal(s, t):
        r = lax.broadcasted_iota(jnp.int32, s.shape, 0)
        c = lax.broadcasted_iota(jnp.int32, s.shape, 1)
        return jnp.where(r - c <= t, s, NEG_BIG)

    def scores(hh, q0, j, c0=0, c1=TQ, r0=0, r1=TK, koff=None):
        ks = pl.multiple_of(j * TK + r0, QW)
        k = k_ref[0, hh, pl.ds(ks, r1 - r0), :]
        q = q_ref[0, hh, pl.ds(pl.multiple_of(q0 + c0, QW), c1 - c0), :]
        s = _dot_nt(k, q)
        if koff is not None:
            s = causal(s, c0 - r0 - koff)
        s_sc[hh, r0:r1, c0:c1] = s
        cmax_sc[hh, :, c0:c1] = jnp.max(s, axis=0, keepdims=True)

    def softmax_pv(hh, j, c0=0, c1=TQ, r0=0, r1=TK, koff_here=None):
        ks = pl.multiple_of(j * TK + r0, QW)
        s = s_sc[hh, r0:r1, c0:c1]
        if koff_here is not None:
            s = causal(s, c0 - r0 - koff_here)
            m_cur = jnp.max(s, axis=0, keepdims=True)
        else:
            m_cur = cmax_sc[hh, :, c0:c1]
        m_prev = m_sc[hh, :, c0:c1]
        m_next = jnp.maximum(m_prev, m_cur)
        p = jnp.exp2(s - m_next).astype(_bf16)
        alpha = jnp.exp2(m_prev - m_next)
        vt = vt_ref[0, hh, :, pl.ds(ks, r1 - r0)]
        acc_sc[hh, :, c0:c1] = alpha * acc_sc[hh, :, c0:c1] + _dot(vt, p)
        m_sc[hh, :, c0:c1] = m_next

    def diag_quarters(koff):
        out = []
        for c0 in range(0, TQ, QW):
            last_key = c0 + QW - 1 - koff
            if last_key < 0:
                continue
            r1 = min(TK, (last_key // QW + 1) * QW)
            out.append((c0, c0 + QW, r1, c0 - koff < r1 - 1))
        return out

    scores(0, 0, 0)

    def q_tile(qi, carry):
        q0 = pl.multiple_of(qi * TQ, TQ)
        m_sc[...] = jnp.full(m_sc.shape, NEG_BIG, _f32)
        acc_sc[...] = jnp.zeros(acc_sc.shape, _f32)

        def step(nxt, cur):
            for c0 in range(0, TQ, QW):
                scores(nxt[0], q0, nxt[1], c0, c0 + QW)
                softmax_pv(cur[0], cur[1], c0, c0 + QW)

        def one_block(j):
            step((1, j), (0, j))
            step((0, j + 1), (1, j))

        def two_blocks(i, c):
            one_block(2 * i)
            one_block(2 * i + 1)
            return c

        def last_block(i, c):
            one_block(2 * (nb // 2))
            return c

        nb = (TQ // TK) * qi
        lax.fori_loop(0, nb // 2, two_blocks, 0)
        lax.fori_loop(0, nb % 2, last_block, 0)
        q_next = jnp.minimum(qi + 1, nq - 1) * TQ
        full = [(c0, c0 + QW, TK, False) for c0 in range(0, TQ, QW)]
        units = [dict(hh=hh, q0=q0, j=nb + d, koff=d * TK, parts=diag_quarters(d * TK),
                      stored_masked=(d, hh) != (0, 0))
                 for d in range(TQ // TK) for hh in range(HEADS_PER_STEP)]
        units.append(dict(hh=0, q0=q_next, j=0, koff=None, parts=full, stored_masked=True))
        for cur, nxt in zip(units[:-1], units[1:]):
            for i in range(max(len(cur["parts"]), len(nxt["parts"]))):
                if i < len(nxt["parts"]):
                    c0, c1, r1, masked = nxt["parts"][i]
                    scores(nxt["hh"], nxt["q0"], nxt["j"], c0, c1, 0, r1,
                           nxt["koff"] if masked else None)
                if i < len(cur["parts"]):
                    c0, c1, r1, masked = cur["parts"][i]
                    mask_now = masked and not cur["stored_masked"]
                    softmax_pv(cur["hh"], cur["j"], c0, c1, 0, r1,
                               cur["koff"] if mask_now else None)

        acc0, acc1 = acc_sc[0], acc_sc[1]
        o0 = acc0 * (1.0 / acc0[ONES_ROW[0]:ONES_ROW[0] + 1, :])
        o1 = acc1 * (1.0 / acc1[ONES_ROW[1]:ONES_ROW[1] + 1, :])
        row = lax.broadcasted_iota(jnp.int32, (LANES, TQ), 0)
        both = jnp.where(row < V_DIM, o0, o1).T
        rows = pl.ds(q0, TQ)
        o_ref[0, rows, :] = (both * g_ref[0, rows, :].astype(_f32)).astype(_bf16)
        return carry

    lax.fori_loop(0, nq, q_tile, 0)


def _out_kernel(x_ref, oa_ref, ob_ref, wo_ref, g_ref, b_ref, o_ref):
    for c in range(TM_OUT // OUT_CHUNK):
        rs = slice(c * OUT_CHUNK, (c + 1) * OUT_CHUNK)
        y = _dot(oa_ref[rs, :], wo_ref[:A_WIDTH, :]) + _dot(ob_ref[rs, :], wo_ref[A_WIDTH:, :])
        z = DN_ALPHA * x_ref[rs, :] + y
        mu = jnp.mean(z, axis=-1, keepdims=True)
        zc = z - mu
        var = jnp.mean(zc * zc, axis=-1, keepdims=True)
        o_ref[rs, :] = zc * lax.rsqrt(var + EPS) * g_ref[...] + b_ref[...]


def _prep_weights(w_in, w_uq, w_ukv, w_spatial, b_spatial):
    kr0 = Q_LORA + KV_LORA
    zeros = lambda n: jnp.zeros((D_MODEL, n), w_in.dtype)
    win_p = jnp.concatenate([w_in[:, :kr0], zeros(ROPE_LO), w_in[:, kr0:kr0 + ROPE],
                             zeros(LANES - ROPE_LO - ROPE), w_in[:, kr0 + ROPE:]],
                            axis=1).astype(_bf16)

    wq = w_uq.reshape(Q_LORA, HEADS, NOPE + ROPE)
    wq = jnp.pad(wq, ((0, 0), (0, 0), (0, LANES - NOPE - ROPE))).reshape(Q_LORA, HP)
    wkv = w_ukv.reshape(KV_LORA, HEADS, NOPE + V_DIM)
    wk = wkv[:, :, :NOPE].reshape(KV_LORA, HEADS * NOPE)
    wvt = wkv[:, :, NOPE:].reshape(KV_LORA, A_WIDTH).T

    wsp = w_spatial.reshape(G_HEADS // 2, 2, CHUNK, CHUNK).transpose(0, 2, 1, 3)
    wsp = wsp.reshape(G_HEADS // 2, CHUNK, 2 * CHUNK)
    bsp = jnp.repeat(b_spatial.T, G_HEAD_DIM, axis=1)
    return win_p, wq.astype(_bf16), wk.astype(_bf16), wvt.astype(_bf16), wsp, bsp


def _aux_table():
    inv_freq = 1.0 / (ROPE_THETA ** (jnp.arange(HALF, dtype=_f32) / HALF))
    freq = jnp.tile(inv_freq, POS_PER_ROW)
    return jnp.concatenate([freq[None], jnp.zeros((7, LANES), _f32)], axis=0)


def _pack_positions(positions):
    b, s = positions.shape
    return jnp.repeat(positions.reshape(b, s // POS_PER_ROW, POS_PER_ROW), HALF, axis=-1)


def kernel(x, positions, w_in, q_norm_g, w_uq, kv_norm_g, w_ukv, sgu_norm_g, sgu_norm_b,
           w_spatial, b_spatial, w_out, ln_g, ln_b):
    b, s, d = x.shape
    assert d == D_MODEL and s % TQ == 0 and s % TM_PROJ == 0 and TQ % TK == 0 and HEADS_PER_STEP == 2
    win_p, wq, wk, wvt, wsp, bsp = _prep_weights(w_in, w_uq, w_ukv, w_spatial, b_spatial)
    aux = _aux_table()
    pos8 = _pack_positions(positions)

    const2 = lambda shape: pl.BlockSpec(shape, lambda bi, i: (0, 0))
    const3 = lambda shape: pl.BlockSpec(shape, lambda bi, i: (0, 0, 0))
    head_spec = pl.BlockSpec((1, HEADS, TM_PROJ, LANES), lambda bi, i: (bi, 0, i, 0))
    row_spec = lambda w: pl.BlockSpec((1, TM_PROJ, w), lambda bi, i: (bi, i, 0))
    hshape = jax.ShapeDtypeStruct((b, HEADS, s, LANES), _bf16)
    qp, kp, vtp, ga, ob = pl.pallas_call(
        _proj_kernel,
        grid=(b, s // TM_PROJ),
        in_specs=[row_spec(D_MODEL),
                  pl.BlockSpec((1, TM_PROJ // POS_PER_ROW, LANES), lambda bi, i: (bi, i, 0)),
                  const2((8, LANES)),
                  const2((D_MODEL, C_END)), const2((1, Q_LORA)), const2((Q_LORA, HP)),
                  const2((1, KV_LORA)), const2((KV_LORA, HEADS * NOPE)), const2((A_WIDTH, KV_LORA)),
                  const2((1, G_WIDTH)), const2((1, G_WIDTH)),
                  const3((G_HEADS // 2, CHUNK, 2 * CHUNK)), const2((CHUNK, G_WIDTH))],
        out_specs=[head_spec, head_spec,
                   pl.BlockSpec((1, HEADS, LANES, TM_PROJ), lambda bi, i: (bi, 0, 0, i)),
                   row_spec(A_WIDTH), row_spec(G_WIDTH)],
        out_shape=[hshape, hshape, jax.ShapeDtypeStruct((b, HEADS, LANES, s), _bf16),
                   jax.ShapeDtypeStruct((b, s, A_WIDTH), _bf16),
                   jax.ShapeDtypeStruct((b, s, G_WIDTH), _bf16)],
        scratch_shapes=[pltpu.VMEM((TM_PROJ, LANES), _f32)] * 3,
        compiler_params=pltpu.CompilerParams(
            dimension_semantics=("arbitrary", "arbitrary"), vmem_limit_bytes=VMEM_LIMIT),
        name="hyb_proj",
    )(x, pos8, aux, win_p, q_norm_g.reshape(1, -1), wq, kv_norm_g.reshape(1, -1), wk, wvt,
      sgu_norm_g.reshape(1, -1), sgu_norm_b.reshape(1, -1), wsp, bsp)

    hps = HEADS_PER_STEP
    per_pair = pl.BlockSpec((1, hps, s, LANES), lambda bi, hp: (bi, hp, 0, 0))
    gate_spec = pl.BlockSpec((1, s, LANES), lambda bi, hp: (bi, 0, hp))
    oa = pl.pallas_call(
        _attn_kernel,
        grid=(b, HEADS // hps),
        in_specs=[per_pair, per_pair,
                  pl.BlockSpec((1, hps, LANES, s), lambda bi, hp: (bi, hp, 0, 0)), gate_spec],
        out_specs=gate_spec,
        out_shape=jax.ShapeDtypeStruct((b, s, A_WIDTH), _bf16),
        scratch_shapes=[pltpu.VMEM((hps, TK, TQ), _f32), pltpu.VMEM((hps, 1, TQ), _f32),
                        pltpu.VMEM((hps, 1, TQ), _f32), pltpu.VMEM((hps, LANES, TQ), _f32)],
        compiler_params=pltpu.CompilerParams(
            dimension_semantics=("arbitrary", "arbitrary"), vmem_limit_bytes=VMEM_LIMIT_ATTN),
        name="hyb_attn",
    )(qp, kp, vtp, ga)

    r = b * s
    rows = lambda w: pl.BlockSpec((TM_OUT, w), lambda i: (i, 0))
    fixed = lambda shape: pl.BlockSpec(shape, lambda i: (0, 0))
    out = pl.pallas_call(
        _out_kernel,
        grid=(r // TM_OUT,),
        in_specs=[rows(D_MODEL), rows(A_WIDTH), rows(G_WIDTH), fixed((D_MODEL, D_MODEL)),
                  fixed((1, D_MODEL)), fixed((1, D_MODEL))],
        out_specs=rows(D_MODEL),
        out_shape=jax.ShapeDtypeStruct((r, D_MODEL), x.dtype),
        compiler_params=pltpu.CompilerParams(
            dimension_semantics=("arbitrary",), vmem_limit_bytes=VMEM_LIMIT),
        name="hyb_out",
    )(x.reshape(r, d), oa.reshape(r, A_WIDTH), ob.reshape(r, G_WIDTH), w_out.astype(_bf16),
      ln_g.reshape(1, -1), ln_b.reshape(1, -1))
    return out.reshape(b, s, d)
```

```python
import math

import jax
import jax.numpy as jnp
from jax import lax
from jax.experimental import pallas as pl
from jax.experimental.pallas import tpu as pltpu

D_MODEL = 1024
HEADS = 8
NOPE = 64
ROPE = 32
HALF = ROPE // 2
POS_PER_ROW = 8
V_DIM = 64
ONES_ROW = (V_DIM, 0)
A_WIDTH = HEADS * V_DIM
Q_LORA = 256
KV_LORA = 128
ROPE_THETA = 10000.0
CHUNK = 128
G_HEADS = 8
G_WIDTH = 512
G_HEAD_DIM = G_WIDTH // G_HEADS
DN_ALPHA = 2.0 ** 0.25
EPS = 1e-5
SM_SCALE = 1.0 / math.sqrt(NOPE + ROPE)
Q_SCALE = SM_SCALE * math.log2(math.e)

LANES = 128
HP = HEADS * LANES
ROPE_LO = NOPE

C_CQ = 0
C_CKV = C_CQ + Q_LORA
C_KR = C_CKV + KV_LORA
C_ZA = C_KR + LANES
C_U = C_ZA + A_WIDTH
C_V = C_U + G_WIDTH
C_ZB = C_V + G_WIDTH
C_END = C_ZB + G_WIDTH

TM_PROJ = 512
PROJ_CHUNK = 512
TM_OUT = 2048
OUT_CHUNK = 256
TQ = 2048
TK = 1024
QW = 256
HEADS_PER_STEP = 2
NEG_BIG = -1e30

VMEM_LIMIT = 48 * 1024 * 1024
VMEM_LIMIT_ATTN = 56 * 1024 * 1024

_f32 = jnp.float32
_bf16 = jnp.bfloat16


def _dot(a, b):
    return jnp.dot(a, b, preferred_element_type=_f32)


def _dot_nt(a, b):
    return lax.dot_general(a, b, (((1,), (1,)), ((), ())), preferred_element_type=_f32)


def _gelu(x):
    return 0.5 * x * (1.0 + lax.erf(x * (1.0 / math.sqrt(2.0))))


def _silu(x):
    return x * (1.0 / (1.0 + jnp.exp(-x)))


def _proj_kernel(x_ref, pos_ref, aux_ref, win_ref, gq_ref, wuq_ref, gkv_ref, wuk_ref,
                 wuvt_ref, sg_ref, sb_ref, wsp_ref, bsp_ref,
                 q_out, k_out, vt_out, ga_out, ob_out, cos_sc, sup_sc, sdn_sc):
    tm = x_ref.shape[1]

    ang = pos_ref[0].astype(_f32) * aux_ref[0:1, :]
    cos8 = jnp.cos(ang)
    sin8 = jnp.sin(ang)
    lane = lax.broadcasted_iota(jnp.int32, cos8.shape, 1)
    in_lo = (lane >= ROPE_LO) & (lane < ROPE_LO + HALF)
    in_hi = (lane >= ROPE_LO + HALF) & (lane < ROPE_LO + ROPE)

    def place(t8, lane0, i):
        shift = (lane0 - HALF * i) % LANES
        return t8 if shift == 0 else pltpu.roll(t8, shift, 1)

    for i in range(POS_PER_ROW):
        rows = pl.ds(i, tm // POS_PER_ROW, stride=POS_PER_ROW)
        cos_sc[rows, :] = jnp.where(in_lo, place(cos8, ROPE_LO, i),
                                    jnp.where(in_hi, place(cos8, ROPE_LO + HALF, i), 1.0))
        sup_sc[rows, :] = jnp.where(in_hi, place(sin8, ROPE_LO + HALF, i), 0.0)
        sdn_sc[rows, :] = jnp.where(in_lo, -place(sin8, ROPE_LO, i), 0.0)

    def rms(c, g_ref):
        return c * lax.rsqrt(jnp.mean(c * c, axis=-1, keepdims=True) + EPS) * g_ref[...]

    row = lax.broadcasted_iota(jnp.int32, (CHUNK, 2 * CHUNK), 0)
    col = lax.broadcasted_iota(jnp.int32, (CHUNK, 2 * CHUNK), 1)
    tri = (col % CHUNK) <= row
    lo_half = lax.broadcasted_iota(jnp.int32, (CHUNK, LANES), 1) < G_HEAD_DIM
    vt_row = lax.broadcasted_iota(jnp.int32, (LANES, PROJ_CHUNK), 0)
    own_rows = [vt_row < V_DIM, vt_row >= V_DIM]
    nope_lanes = lax.broadcasted_iota(jnp.int32, (PROJ_CHUNK, LANES), 1) < NOPE
    fill_rows = [jnp.where(vt_row == ONES_ROW[par], 1.0, 0.0) for par in range(2)]

    for rc in range(tm // PROJ_CHUNK):
        rs = slice(rc * PROJ_CHUNK, (rc + 1) * PROJ_CHUNK)
        x = x_ref[0, rs, :].astype(_bf16)

        def proj(lo, hi, x=x):
            return _dot(x, win_ref[:, lo:hi])

        def rope(t, rs=rs):
            return (t * cos_sc[rs, :] + pltpu.roll(t, HALF, 1) * sup_sc[rs, :]
                    + pltpu.roll(t, LANES - HALF, 1) * sdn_sc[rs, :])

        lat = proj(C_CQ, C_ZA)
        v_pre = proj(C_V, C_ZB)
        u_pre = proj(C_U, C_V)
        cqn = rms(lat[:, C_CQ:C_CKV], gq_ref).astype(_bf16)
        q = _dot(cqn, wuq_ref[...]) * Q_SCALE
        ckn = rms(lat[:, C_CKV:C_KR], gkv_ref).astype(_bf16)
        kn = _dot(ckn, wuk_ref[...])
        vt = _dot_nt(wuvt_ref[...], ckn)
        kr = rope(lat[:, C_KR:C_ZA])
        for h in range(HEADS):
            sl = slice(h * LANES, (h + 1) * LANES)
            q_out[0, h, rs, :] = rope(q[:, sl]).astype(_bf16)
            kn_pair = kn[:, (h // 2) * LANES:(h // 2 + 1) * LANES]
            kn_h = kn_pair if h % 2 == 0 else pltpu.roll(kn_pair, NOPE, 1)
            k_out[0, h, rs, :] = (jnp.where(nope_lanes, kn_h, 0.0) + kr).astype(_bf16)
            vt_pair = vt[(h // 2) * LANES:(h // 2 + 1) * LANES, :]
            vt_out[0, h, :, rs] = jnp.where(own_rows[h % 2], vt_pair, fill_rows[h % 2]).astype(_bf16)

        zb_pre = proj(C_ZB, C_END)
        ga_out[0, rs, :] = _silu(proj(C_ZA, C_U)).astype(_bf16)

        u = _gelu(u_pre)
        v = _gelu(v_pre)
        mu = jnp.mean(v, axis=-1, keepdims=True)
        vc = v - mu
        var = jnp.mean(vc * vc, axis=-1, keepdims=True)
        vn = (vc * lax.rsqrt(var + EPS) * sg_ref[...] + sb_ref[...])
        gb = _silu(zb_pre)

        for pair in range(G_HEADS // 2):
            w_pair = jnp.where(tri, wsp_ref[pair], 0.0).astype(_bf16)
            ls = slice(pair * LANES, (pair + 1) * LANES)
            n_chunks = PROJ_CHUNK // CHUNK
            stacked = []
            for c in range(n_chunks):
                vp = vn[c * CHUNK:(c + 1) * CHUNK, ls]
                stacked.append(jnp.concatenate(
                    [jnp.where(lo_half, vp, 0.0), jnp.where(lo_half, 0.0, vp)], axis=0
                ).astype(_bf16))
            sv_all = _dot(w_pair, jnp.concatenate(stacked, axis=1))
            for c in range(n_chunks):
                cs = slice(c * CHUNK, (c + 1) * CHUNK)
                os = slice(rc * PROJ_CHUNK + c * CHUNK, rc * PROJ_CHUNK + (c + 1) * CHUNK)
                sv = sv_all[:, c * LANES:(c + 1) * LANES] + bsp_ref[:, ls]
                ob_out[0, os, ls] = (u[cs, ls] * sv * gb[cs, ls]).astype(_bf16)


def _attn_kernel(q_ref, k_ref, vt_ref, g_ref, o_ref, s_sc, cmax_sc, m_sc, acc_sc):
    seq = q_ref.shape[2]
    nq = seq // TQ

    def causal(s, t):
        r = lax.broadcasted_iota(jnp.int32, s.shape, 0)
        c = lax.broadcasted_iota(jnp.int32, s.shape, 1)
        return jnp.where(r - c <= t, s, NEG_BIG)

    def scores(hh, q0, j, c0=0, c1=TQ, r0=0, r1=TK, koff=None):
        ks = pl.multiple_of(j * TK + r0, QW)
        k = k_ref[0, hh, pl.ds(ks, r1 - r0), :]
        q = q_ref[0, hh, pl.ds(pl.multiple_of(q0 + c0, QW), c1 - c0), :]
        s = _dot_nt(k, q)
        if koff is not None:
            s = causal(s, c0 - r0 - koff)
        s_sc[hh, r0:r1, c0:c1] = s
        cmax_sc[hh, :, c0:c1] = jnp.max(s, axis=0, keepdims=True)

    def softmax_pv(hh, j, c0=0, c1=TQ, r0=0, r1=TK, koff_here=None):
        ks = pl.multiple_of(j * TK + r0, QW)
        s = s_sc[hh, r0:r1, c0:c1]
        if koff_here is not None:
            s = causal(s, c0 - r0 - koff_here)
            m_cur = jnp.max(s, axis=0, keepdims=True)
        else:
            m_cur = cmax_sc[hh, :, c0:c1]
        m_prev = m_sc[hh, :, c0:c1]
        m_next = jnp.maximum(m_prev, m_cur)
        p = jnp.exp2(s - m_next).astype(_bf16)
        alpha = jnp.exp2(m_prev - m_next)
        vt = vt_ref[0, hh, :, pl.ds(ks, r1 - r0)]
        acc_sc[hh, :, c0:c1] = alpha * acc_sc[hh, :, c0:c1] + _dot(vt, p)
        m_sc[hh, :, c0:c1] = m_next

    def diag_quarters(koff):
        out = []
        for c0 in range(0, TQ, QW):
            last_key = c0 + QW - 1 - koff
            if last_key < 0:
                continue
            r1 = min(TK, (last_key // QW + 1) * QW)
            out.append((c0, c0 + QW, r1, c0 - koff < r1 - 1))
        return out

    scores(0, 0, 0)

    def q_tile(qi, carry):
        q0 = pl.multiple_of(qi * TQ, TQ)
        m_sc[...] = jnp.full(m_sc.shape, NEG_BIG, _f32)
        acc_sc[...] = jnp.zeros(acc_sc.shape, _f32)

        def step(nxt, cur):
            for c0 in range(0, TQ, QW):
                scores(nxt[0], q0, nxt[1], c0, c0 + QW)
                softmax_pv(cur[0], cur[1], c0, c0 + QW)

        def one_block(j):
            step((1, j), (0, j))
            step((0, j + 1), (1, j))

        def two_blocks(i, c):
            one_block(2 * i)
            one_block(2 * i + 1)
            return c

        def last_block(i, c):
            one_block(2 * (nb // 2))
            return c

        nb = (TQ // TK) * qi
        lax.fori_loop(0, nb // 2, two_blocks, 0)
        lax.fori_loop(0, nb % 2, last_block, 0)
        q_next = jnp.minimum(qi + 1, nq - 1) * TQ
        full = [(c0, c0 + QW, TK, False) for c0 in range(0, TQ, QW)]
        units = [dict(hh=hh, q0=q0, j=nb + d, koff=d * TK, parts=diag_quarters(d * TK),
                      stored_masked=(d, hh) != (0, 0))
                 for d in range(TQ // TK) for hh in range(HEADS_PER_STEP)]
        units.append(dict(hh=0, q0=q_next, j=0, koff=None, parts=full, stored_masked=True))
        for cur, nxt in zip(units[:-1], units[1:]):
            for i in range(max(len(cur["parts"]), len(nxt["parts"]))):
                if i < len(nxt["parts"]):
                    c0, c1, r1, masked = nxt["parts"][i]
                    scores(nxt["hh"], nxt["q0"], nxt["j"], c0, c1, 0, r1,
                           nxt["koff"] if masked else None)
                if i < len(cur["parts"]):
                    c0, c1, r1, masked = cur["parts"][i]
                    mask_now = masked and not cur["stored_masked"]
                    softmax_pv(cur["hh"], cur["j"], c0, c1, 0, r1,
                               cur["koff"] if mask_now else None)

        acc0, acc1 = acc_sc[0], acc_sc[1]
        o0 = acc0 * (1.0 / acc0[ONES_ROW[0]:ONES_ROW[0] + 1, :])
        o1 = acc1 * (1.0 / acc1[ONES_ROW[1]:ONES_ROW[1] + 1, :])
        row = lax.broadcasted_iota(jnp.int32, (LANES, TQ), 0)
        both = jnp.where(row < V_DIM, o0, o1).T
        rows = pl.ds(q0, TQ)
        o_ref[0, rows, :] = (both * g_ref[0, rows, :].astype(_f32)).astype(_bf16)
        return carry

    lax.fori_loop(0, nq, q_tile, 0)


def _out_kernel(x_ref, oa_ref, ob_ref, wo_ref, g_ref, b_ref, o_ref):
    for c in range(TM_OUT // OUT_CHUNK):
        rs = slice(c * OUT_CHUNK, (c + 1) * OUT_CHUNK)
        y = _dot(oa_ref[rs, :], wo_ref[:A_WIDTH, :]) + _dot(ob_ref[rs, :], wo_ref[A_WIDTH:, :])
        z = DN_ALPHA * x_ref[rs, :] + y
        mu = jnp.mean(z, axis=-1, keepdims=True)
        zc = z - mu
        var = jnp.mean(zc * zc, axis=-1, keepdims=True)
        o_ref[rs, :] = zc * lax.rsqrt(var + EPS) * g_ref[...] + b_ref[...]


def _prep_weights(w_in, w_uq, w_ukv, w_spatial, b_spatial):
    kr0 = Q_LORA + KV_LORA
    zeros = lambda n: jnp.zeros((D_MODEL, n), w_in.dtype)
    win_p = jnp.concatenate([w_in[:, :kr0], zeros(ROPE_LO), w_in[:, kr0:kr0 + ROPE],
                             zeros(LANES - ROPE_LO - ROPE), w_in[:, kr0 + ROPE:]],
                            axis=1).astype(_bf16)

    wq = w_uq.reshape(Q_LORA, HEADS, NOPE + ROPE)
    wq = jnp.pad(wq, ((0, 0), (0, 0), (0, LANES - NOPE - ROPE))).reshape(Q_LORA, HP)
    wkv = w_ukv.reshape(KV_LORA, HEADS, NOPE + V_DIM)
    wk = wkv[:, :, :NOPE].reshape(KV_LORA, HEADS * NOPE)
    wvt = wkv[:, :, NOPE:].reshape(KV_LORA, A_WIDTH).T

    wsp = w_spatial.reshape(G_HEADS // 2, 2, CHUNK, CHUNK).transpose(0, 2, 1, 3)
    wsp = wsp.reshape(G_HEADS // 2, CHUNK, 2 * CHUNK)
    bsp = jnp.repeat(b_spatial.T, G_HEAD_DIM, axis=1)
    return win_p, wq.astype(_bf16), wk.astype(_bf16), wvt.astype(_bf16), wsp, bsp


def _aux_table():
    inv_freq = 1.0 / (ROPE_THETA ** (jnp.arange(HALF, dtype=_f32) / HALF))
    freq = jnp.tile(inv_freq, POS_PER_ROW)
    return jnp.concatenate([freq[None], jnp.zeros((7, LANES), _f32)], axis=0)


def _pack_positions(positions):
    b, s = positions.shape
    return jnp.repeat(positions.reshape(b, s // POS_PER_ROW, POS_PER_ROW), HALF, axis=-1)


def kernel(x, positions, w_in, q_norm_g, w_uq, kv_norm_g, w_ukv, sgu_norm_g, sgu_norm_b,
           w_spatial, b_spatial, w_out, ln_g, ln_b):
    b, s, d = x.shape
    assert d == D_MODEL and s % TQ == 0 and s % TM_PROJ == 0 and TQ % TK == 0 and HEADS_PER_STEP == 2
    win_p, wq, wk, wvt, wsp, bsp = _prep_weights(w_in, w_uq, w_ukv, w_spatial, b_spatial)
    aux = _aux_table()
    pos8 = _pack_positions(positions)

    const2 = lambda shape: pl.BlockSpec(shape, lambda bi, i: (0, 0), pipeline_mode=pl.Buffered(1))
    const3 = lambda shape: pl.BlockSpec(shape, lambda bi, i: (0, 0, 0), pipeline_mode=pl.Buffered(1))
    head_spec = pl.BlockSpec((1, HEADS, TM_PROJ, LANES), lambda bi, i: (bi, 0, i, 0))
    row_spec = lambda w: pl.BlockSpec((1, TM_PROJ, w), lambda bi, i: (bi, i, 0))
    hshape = jax.ShapeDtypeStruct((b, HEADS, s, LANES), _bf16)
    qp, kp, vtp, ga, ob = pl.pallas_call(
        _proj_kernel,
        grid=(b, s // TM_PROJ),
        in_specs=[row_spec(D_MODEL),
                  pl.BlockSpec((1, TM_PROJ // POS_PER_ROW, LANES), lambda bi, i: (bi, i, 0)),
                  const2((8, LANES)),
                  const2((D_MODEL, C_END)), const2((1, Q_LORA)), const2((Q_LORA, HP)),
                  const2((1, KV_LORA)), const2((KV_LORA, HEADS * NOPE)), const2((A_WIDTH, KV_LORA)),
                  const2((1, G_WIDTH)), const2((1, G_WIDTH)),
                  const3((G_HEADS // 2, CHUNK, 2 * CHUNK)), const2((CHUNK, G_WIDTH))],
        out_specs=[head_spec, head_spec,
                   pl.BlockSpec((1, HEADS, LANES, TM_PROJ), lambda bi, i: (bi, 0, 0, i)),
                   row_spec(A_WIDTH), row_spec(G_WIDTH)],
        out_shape=[hshape, hshape, jax.ShapeDtypeStruct((b, HEADS, LANES, s), _bf16),
                   jax.ShapeDtypeStruct((b, s, A_WIDTH), _bf16),
                   jax.ShapeDtypeStruct((b, s, G_WIDTH), _bf16)],
        scratch_shapes=[pltpu.VMEM((TM_PROJ, LANES), _f32)] * 3,
        compiler_params=pltpu.CompilerParams(
            dimension_semantics=("arbitrary", "arbitrary"), vmem_limit_bytes=VMEM_LIMIT),
        name="hyb_proj",
    )(x, pos8, aux, win_p, q_norm_g.reshape(1, -1), wq, kv_norm_g.reshape(1, -1), wk, wvt,
      sgu_norm_g.reshape(1, -1), sgu_norm_b.reshape(1, -1), wsp, bsp)

    hps = HEADS_PER_STEP
    per_pair = pl.BlockSpec((1, hps, s, LANES), lambda bi, hp: (bi, hp, 0, 0))
    gate_spec = pl.BlockSpec((1, s, LANES), lambda bi, hp: (bi, 0, hp))
    oa = pl.pallas_call(
        _attn_kernel,
        grid=(b, HEADS // hps),
        in_specs=[per_pair, per_pair,
                  pl.BlockSpec((1, hps, LANES, s), lambda bi, hp: (bi, hp, 0, 0)), gate_spec],
        out_specs=gate_spec,
        out_shape=jax.ShapeDtypeStruct((b, s, A_WIDTH), _bf16),
        scratch_shapes=[pltpu.VMEM((hps, TK, TQ), _f32), pltpu.VMEM((hps, 1, TQ), _f32),
                        pltpu.VMEM((hps, 1, TQ), _f32), pltpu.VMEM((hps, LANES, TQ), _f32)],
        compiler_params=pltpu.CompilerParams(
            dimension_semantics=("arbitrary", "arbitrary"), vmem_limit_bytes=VMEM_LIMIT_ATTN),
        name="hyb_attn",
    )(qp, kp, vtp, ga)

    r = b * s
    rows = lambda w: pl.BlockSpec((TM_OUT, w), lambda i: (i, 0))
    fixed = lambda shape: pl.BlockSpec(shape, lambda i: (0, 0), pipeline_mode=pl.Buffered(1))
    out = pl.pallas_call(
        _out_kernel,
        grid=(r // TM_OUT,),
        in_specs=[rows(D_MODEL), rows(A_WIDTH), rows(G_WIDTH), fixed((D_MODEL, D_MODEL)),
                  fixed((1, D_MODEL)), fixed((1, D_MODEL))],
        out_specs=rows(D_MODEL),
        out_shape=jax.ShapeDtypeStruct((r, D_MODEL), x.dtype),
        compiler_params=pltpu.CompilerParams(
            dimension_semantics=("arbitrary",), vmem_limit_bytes=VMEM_LIMIT),
        name="hyb_out",
    )(x.reshape(r, d), oa.reshape(r, A_WIDTH), ob.reshape(r, G_WIDTH), w_out.astype(_bf16),
      ln_g.reshape(1, -1), ln_b.reshape(1, -1))
    return out.reshape(b, s, d)
```
